```python
import math
import jax, jax.numpy as jnp
from jax import lax
import numpy as np

D_MODEL = 2048
BATCH = 4
SEQ = 8192
DEPTH = 1

N_META = 16
LN_EPS = 1e-5
CONV_CH = D_MODEL
CONV_TAPS = 31
N_HEADS = 16
HEAD_DIM = D_MODEL // N_HEADS
ATTN_WIDTH = N_HEADS * HEAD_DIM
Q_BLOCK = 128
PEER_HEADS = 8
PEER_NKEYS = 128
PEER_EXPERTS = PEER_NKEYS * PEER_NKEYS
PEER_DQ = 256
PEER_DSUB = PEER_DQ // 2
PEER_TOPK = 16
PEER_CHUNK = 128
DN_ALPHA = (2.0 * DEPTH) ** 0.25
DN_BETA = (8.0 * DEPTH) ** -0.25

OFF_CONV_A = 0
OFF_CONV_G = OFF_CONV_A + CONV_CH
OFF_Q = OFF_CONV_G + CONV_CH
OFF_K = OFF_Q + ATTN_WIDTH
OFF_V = OFF_K + ATTN_WIDTH
OFF_F = OFF_V + ATTN_WIDTH
OFF_GATE_C = OFF_F + N_HEADS
OFF_GATE_A = OFF_GATE_C + D_MODEL
N_IN = OFF_GATE_A + D_MODEL

kernel_name = "hybrid_conformer_fox_peer_block"


def layer_norm(x, g, b):
    xf = x.astype(jnp.float32)
    mu = jnp.mean(xf, axis=-1, keepdims=True)
    var = jnp.mean(jnp.square(xf - mu), axis=-1, keepdims=True)
    return ((xf - mu) * lax.rsqrt(var + LN_EPS)).astype(x.dtype) * g + b


def conformer_conv(a, gate, dw_w, dw_b, ln_g, ln_b, w_o):
    h = a * jax.nn.sigmoid(gate)
    h = lax.conv_general_dilated(
        h, dw_w[:, None, :].astype(h.dtype), window_strides=(1,),
        padding=((CONV_TAPS - 1, 0),),
        dimension_numbers=('NWC', 'WIO', 'NWC'),
        feature_group_count=CONV_CH) + dw_b
    h = jax.nn.silu(layer_norm(h, ln_g, ln_b))
    return h @ w_o


def forgetting_attention(q, k, v, logf, w_o):
    B, L = q.shape[0], q.shape[1]
    scale = HEAD_DIM ** -0.5
    c = jnp.cumsum(logf, axis=1)
    c_k = jnp.transpose(c, (0, 2, 1))
    kpos = jnp.arange(L)

    def attend(q_blk, c_blk, qpos):
        s = jnp.einsum('bqhd,bkhd->bhqk', q_blk, k,
                       preferred_element_type=jnp.float32) * scale
        s = s + (jnp.transpose(c_blk, (0, 2, 1))[:, :, :, None] - c_k[:, :, None, :])
        s = jnp.where(kpos[None, None, None, :] <= qpos[None, None, :, None], s, -jnp.inf)
        p = jax.nn.softmax(s, axis=-1).astype(v.dtype)
        return jnp.einsum('bhqk,bkhd->bqhd', p, v)

    o_meta = attend(q[:, :N_META], c[:, :N_META], jnp.arange(N_META))
    n_real = L - N_META
    n_blk = n_real // Q_BLOCK
    qr = q[:, N_META:].reshape(B, n_blk, Q_BLOCK, N_HEADS, HEAD_DIM).transpose(1, 0, 2, 3, 4)
    cr = c[:, N_META:].reshape(B, n_blk, Q_BLOCK, N_HEADS).transpose(1, 0, 2, 3)
    pr = (N_META + jnp.arange(n_real)).reshape(n_blk, Q_BLOCK)
    o_r = lax.map(lambda a: attend(a[0], a[1], a[2]), (qr, cr, pr))
    o_r = o_r.transpose(1, 0, 2, 3, 4).reshape(B, n_real, N_HEADS, HEAD_DIM)
    o = jnp.concatenate([o_meta, o_r], axis=1).reshape(B, L, ATTN_WIDTH)
    return o @ w_o


def peer(x, w_q, subkeys, u_tab, v_tab):
    B, L, D = x.shape
    T = B * L
    xt = x.reshape(T, D)
    q = (xt @ w_q).reshape(T, PEER_HEADS, 2, PEER_DSUB)
    s = jnp.einsum('thpd,hpnd->thpn', q, subkeys, preferred_element_type=jnp.float32)
    sv, si = lax.top_k(s, PEER_TOPK)
    cand = sv[:, :, 0, :, None] + sv[:, :, 1, None, :]
    cand_id = si[:, :, 0, :, None] * PEER_NKEYS + si[:, :, 1, None, :]
    cand = cand.reshape(T, PEER_HEADS, PEER_TOPK * PEER_TOPK)
    cand_id = cand_id.reshape(T, PEER_HEADS, PEER_TOPK * PEER_TOPK)
    top_s, top_pos = lax.top_k(cand, PEER_TOPK)
    ids = jnp.take_along_axis(cand_id, top_pos, axis=-1)
    g = jax.nn.softmax(top_s, axis=-1).astype(x.dtype)

    n_chunk = -(-T // PEER_CHUNK)
    pad = n_chunk * PEER_CHUNK - T
    xc_all = jnp.pad(xt, ((0, pad), (0, 0))).reshape(n_chunk, PEER_CHUNK, D)
    ic_all = jnp.pad(ids, ((0, pad), (0, 0), (0, 0))).reshape(n_chunk, PEER_CHUNK, PEER_HEADS, PEER_TOPK)
    gc_all = jnp.pad(g, ((0, pad), (0, 0), (0, 0))).reshape(n_chunk, PEER_CHUNK, PEER_HEADS, PEER_TOPK)

    def expert_chunk(a):
        xc, ic, gc = a
        u = u_tab[ic]
        act = jax.nn.gelu(jnp.einsum('cd,chkd->chk', xc, u), approximate=False)
        vv = v_tab[ic]
        return jnp.einsum('chk,chkd->cd', gc * act, vv)

    out = lax.map(expert_chunk, (xc_all, ic_all, gc_all))
    return out.reshape(n_chunk * PEER_CHUNK, D)[:T].reshape(B, L, D)


def setup_inputs(seed: int = 0) -> dict:
    key = jax.random.key(seed)
    ks = jax.random.split(key, 24)
    f32 = jnp.float32
    nrm = lambda k, shp: jax.random.normal(k, shp, f32)
    x = nrm(ks[0], (BATCH, SEQ, D_MODEL))
    meta_tokens = nrm(ks[1], (N_META, D_MODEL))
    ln_in_g = 1.0 + 0.02 * nrm(ks[2], (D_MODEL,))
    ln_in_b = 0.02 * nrm(ks[3], (D_MODEL,))
    col_scale = np.ones((N_IN,), np.float32)
    col_scale[OFF_V:OFF_F] = DN_BETA
    w_in = nrm(ks[4], (DEPTH, D_MODEL, N_IN)) * (D_MODEL ** -0.5) * jnp.asarray(col_scale)
    b_in = 0.02 * nrm(ks[5], (DEPTH, N_IN))
    b_in = b_in.at[:, OFF_F:OFF_GATE_C].set(
        jax.random.uniform(ks[6], (DEPTH, N_HEADS), f32, minval=1.0, maxval=6.0))
    conv_dw_w = nrm(ks[7], (DEPTH, CONV_TAPS, CONV_CH)) * (CONV_TAPS ** -0.5)
    conv_dw_b = 0.02 * nrm(ks[8], (DEPTH, CONV_CH))
    conv_ln_g = 1.0 + 0.02 * nrm(ks[9], (DEPTH, CONV_CH))
    conv_ln_b = 0.02 * nrm(ks[10], (DEPTH, CONV_CH))
    w_conv_out = nrm(ks[11], (DEPTH, CONV_CH, D_MODEL)) * (CONV_CH ** -0.5) * DN_BETA
    w_attn_out = nrm(ks[12], (DEPTH, ATTN_WIDTH, D_MODEL)) * (ATTN_WIDTH ** -0.5) * DN_BETA
    w_out = nrm(ks[13], (DEPTH, D_MODEL, D_MODEL)) * (D_MODEL ** -0.5) * DN_BETA
    ln1_g = 1.0 + 0.02 * nrm(ks[14], (DEPTH, D_MODEL))
    ln1_b = 0.02 * nrm(ks[15], (DEPTH, D_MODEL))
    peer_w_q = nrm(ks[16], (DEPTH, D_MODEL, PEER_HEADS * PEER_DQ)) * (D_MODEL ** -0.5)
    peer_subkeys = nrm(ks[17], (DEPTH, PEER_HEADS, 2, PEER_NKEYS, PEER_DSUB)) * (PEER_DSUB ** -0.5)
    peer_u = nrm(ks[18], (DEPTH, PEER_EXPERTS, D_MODEL)) * (D_MODEL ** -0.5) * DN_BETA
    peer_v = nrm(ks[19], (DEPTH, PEER_EXPERTS, D_MODEL)) * DN_BETA
    ln2_g = 1.0 + 0.02 * nrm(ks[20], (DEPTH, D_MODEL))
    ln2_b = 0.02 * nrm(ks[21], (DEPTH, D_MODEL))
    return {"x": x, "meta_tokens": meta_tokens, "ln_in_g": ln_in_g, "ln_in_b": ln_in_b,
            "w_in": w_in, "b_in": b_in, "conv_dw_w": conv_dw_w, "conv_dw_b": conv_dw_b,
            "conv_ln_g": conv_ln_g, "conv_ln_b": conv_ln_b, "w_conv_out": w_conv_out,
            "w_attn_out": w_attn_out, "w_out": w_out, "ln1_g": ln1_g, "ln1_b": ln1_b,
            "peer_w_q": peer_w_q, "peer_subkeys": peer_subkeys, "peer_u": peer_u,
            "peer_v": peer_v, "ln2_g": ln2_g, "ln2_b": ln2_b}


def reference(x, meta_tokens, ln_in_g, ln_in_b, w_in, b_in, conv_dw_w, conv_dw_b,
              conv_ln_g, conv_ln_b, w_conv_out, w_attn_out, w_out, ln1_g, ln1_b,
              peer_w_q, peer_subkeys, peer_u, peer_v, ln2_g, ln2_b):
    B = x.shape[0]
    meta = jnp.broadcast_to(meta_tokens[None].astype(x.dtype), (B, N_META, D_MODEL))
    h = jnp.concatenate([meta, x], axis=1)
    h = layer_norm(h, ln_in_g, ln_in_b)
    L = h.shape[1]
    for l in range(DEPTH):
        z = h @ w_in[l] + b_in[l]
        y_conv = conformer_conv(z[..., OFF_CONV_A:OFF_CONV_G], z[..., OFF_CONV_G:OFF_Q],
                                conv_dw_w[l], conv_dw_b[l], conv_ln_g[l], conv_ln_b[l],
                                w_conv_out[l])
        q = z[..., OFF_Q:OFF_K].reshape(B, L, N_HEADS, HEAD_DIM)
        k = z[..., OFF_K:OFF_V].reshape(B, L, N_HEADS, HEAD_DIM)
        v = z[..., OFF_V:OFF_F].reshape(B, L, N_HEADS, HEAD_DIM)
        logf = jax.nn.log_sigmoid(z[..., OFF_F:OFF_GATE_C].astype(jnp.float32))
        y_attn = forgetting_attention(q, k, v, logf, w_attn_out[l])
        g_c = jax.nn.sigmoid(z[..., OFF_GATE_C:OFF_GATE_A])
        g_a = jax.nn.sigmoid(z[..., OFF_GATE_A:N_IN])
        mix = (g_c * y_conv + g_a * y_attn) @ w_out[l]
        h = layer_norm(DN_ALPHA * h + mix, ln1_g[l], ln1_b[l])
        ffn = peer(h, peer_w_q[l], peer_subkeys[l], peer_u[l], peer_v[l])
        h = layer_norm(DN_ALPHA * h + ffn, ln2_g[l], ln2_b[l])
    return h[:, N_META:]
```

```python
import functools
import math

import jax
import jax.numpy as jnp
from jax import lax
from jax.experimental import pallas as pl
from jax.experimental.pallas import tpu as pltpu

F32 = jnp.float32
BF16 = jnp.bfloat16

LN_EPS = 1e-5
N_META = 16
CONV_TAPS = 31
N_HEADS = 16
HEAD_DIM = 128
PEER_HEADS = 8
PEER_NKEYS = 128
PEER_TOPK = 16
DN_ALPHA = 2.0 ** 0.25
LANES = 128
HALO = 32
VMEM_LIMIT = 56 * 1024 * 1024


def _cparams(*sem):
    return pltpu.CompilerParams(dimension_semantics=sem, vmem_limit_bytes=VMEM_LIMIT)


def _ln(x, g, b):
    mu = jnp.mean(x, axis=-1, keepdims=True)
    xc = x - mu
    var = jnp.mean(xc * xc, axis=-1, keepdims=True)
    return xc * lax.rsqrt(var + LN_EPS) * g + b


def _sigmoid(x):
    return 1.0 / (1.0 + jnp.exp(-x))


def _dot(a, b):
    return jnp.dot(a, b, preferred_element_type=F32)


def _dot_nt(a, b):
    return lax.dot_general(a, b, (((1,), (1,)), ((), ())), preferred_element_type=F32)


def _ln_cast_kernel(x_ref, g_ref, b_ref, o_ref):
    o_ref[...] = _ln(x_ref[...], g_ref[...], b_ref[...]).astype(o_ref.dtype)


def _ln_cast(x, g, b, tm):
    m, d = x.shape
    return pl.pallas_call(
        _ln_cast_kernel,
        grid=(m // tm,),
        in_specs=[pl.BlockSpec((tm, d), lambda i: (i, 0)),
                  pl.BlockSpec((1, d), lambda i: (0, 0)),
                  pl.BlockSpec((1, d), lambda i: (0, 0))],
        out_specs=pl.BlockSpec((tm, d), lambda i: (i, 0)),
        out_shape=jax.ShapeDtypeStruct((m, d), BF16),
        compiler_params=_cparams("parallel"),
        name="ln_in",
    )(x, g, b)


def _mm_glu_kernel(x_ref, wa_ref, wg_ref, ba_ref, bg_ref, o_ref):
    x = x_ref[...]
    a = _dot(x, wa_ref[...]) + ba_ref[...]
    g = _dot(x, wg_ref[...]) + bg_ref[...]
    o_ref[...] = (a * _sigmoid(g)).astype(o_ref.dtype)


def _mm_glu(x, w, b, n, tm, tn):
    m, d = x.shape
    nb = n // tn
    return pl.pallas_call(
        _mm_glu_kernel,
        grid=(m // tm, nb),
        in_specs=[pl.BlockSpec((tm, d), lambda i, j: (i, 0)),
                  pl.BlockSpec((d, tn), lambda i, j: (0, j)),
                  pl.BlockSpec((d, tn), lambda i, j: (0, nb + j)),
                  pl.BlockSpec((1, tn), lambda i, j: (0, j)),
                  pl.BlockSpec((1, tn), lambda i, j: (0, nb + j))],
        out_specs=pl.BlockSpec((tm, tn), lambda i, j: (i, j)),
        out_shape=jax.ShapeDtypeStruct((m, n), F32),
        compiler_params=_cparams("parallel", "arbitrary"),
        name="proj_glu",
    )(x, w, w, b, b)


def _mm_heads_kernel(x_ref, w_ref, b_ref, s_ref, o_ref, *, nh):
    acc = (_dot(x_ref[...], w_ref[...]) + b_ref[...]) * s_ref[...]
    for hh in range(nh):
        o_ref[hh] = acc[:, hh * HEAD_DIM:(hh + 1) * HEAD_DIM].astype(o_ref.dtype)


def _mm_heads(x, w, b, scale, col0, n, tm, tn):
    m, d = x.shape
    nh = tn // HEAD_DIM
    cb = col0 // tn
    return pl.pallas_call(
        functools.partial(_mm_heads_kernel, nh=nh),
        grid=(m // tm, n // tn),
        in_specs=[pl.BlockSpec((tm, d), lambda i, j: (i, 0)),
                  pl.BlockSpec((d, tn), lambda i, j: (0, cb + j)),
                  pl.BlockSpec((1, tn), lambda i, j: (0, cb + j)),
                  pl.BlockSpec((1, tn), lambda i, j: (0, j))],
        out_specs=pl.BlockSpec((nh, tm, HEAD_DIM), lambda i, j: (j, i, 0)),
        out_shape=jax.ShapeDtypeStruct((n // HEAD_DIM, m, HEAD_DIM), BF16),
        compiler_params=_cparams("parallel", "arbitrary"),
        name="proj_qkv",
    )(x, w, b, scale)


def _mm_act_kernel(x_ref, w_ref, b_ref, o_ref, *, act):
    y = _dot(x_ref[...], w_ref[...]) + b_ref[...]
    if act == "sigmoid":
        y = _sigmoid(y)
    o_ref[...] = y.astype(o_ref.dtype)


def _mm_act(x, w, b, act, tm, tn, name):
    m, d = x.shape
    n = w.shape[1]
    return pl.pallas_call(
        functools.partial(_mm_act_kernel, act=act),
        grid=(m // tm, n // tn),
        in_specs=[pl.BlockSpec((tm, d), lambda i, j: (i, 0)),
                  pl.BlockSpec((d, tn), lambda i, j: (0, j)),
                  pl.BlockSpec((1, tn), lambda i, j: (0, j))],
        out_specs=pl.BlockSpec((tm, tn), lambda i, j: (i, j)),
        out_shape=jax.ShapeDtypeStruct((m, n), BF16),
        compiler_params=_cparams("parallel", "arbitrary"),
        name=name,
    )(x, w, b)


def _forget_kernel(w_ref, x_ref, b_ref, o_ref):
    z = _dot_nt(w_ref[...], x_ref[...]) + b_ref[...]
    o_ref[...] = jnp.minimum(z, 0.0) - jnp.log1p(jnp.exp(-jnp.abs(z)))


def _forget(wt, x, b, tm):
    m, d = x.shape
    nh = wt.shape[0]
    return pl.pallas_call(
        _forget_kernel,
        grid=(m // tm,),
        in_specs=[pl.BlockSpec((nh, d), lambda i: (0, 0)),
                  pl.BlockSpec((tm, d), lambda i: (i, 0)),
                  pl.BlockSpec((nh, 1), lambda i: (0, 0))],
        out_specs=pl.BlockSpec((nh, tm), lambda i: (0, i)),
        out_shape=jax.ShapeDtypeStruct((nh, m), F32),
        compiler_params=_cparams("parallel"),
        name="forget_logits",
    )(wt, x, b)


def _split_dot(x, tri):
    hi = x.astype(BF16)
    r = x - hi.astype(F32)
    mid = r.astype(BF16)
    lo = (r - mid.astype(F32)).astype(BF16)
    return _dot(hi, tri) + _dot(mid, tri) + _dot(lo, tri)


def _cumsum_kernel(lf_ref, lfm_ref, tri_ref, c_ref, cm_ref, carry_ref, *, cw):
    j = pl.program_id(1)
    cm = _split_dot(lfm_ref[...], tri_ref[0:LANES, 0:LANES])

    @pl.when(j == 0)
    def _():
        carry_ref[...] = cm[:, N_META - 1:N_META]

    c = _split_dot(lf_ref[...], tri_ref[...]) + carry_ref[...]
    c_ref[...] = c
    cm_ref[...] = cm
    carry_ref[...] = c[:, cw - 1:cw]


def _cumsum(lf, lfm_pad, batch, seq, cw):
    nh = lf.shape[0]
    nc = seq // cw
    tri = (lax.broadcasted_iota(jnp.int32, (cw, cw), 0)
           <= lax.broadcasted_iota(jnp.int32, (cw, cw), 1)).astype(BF16)
    return pl.pallas_call(
        functools.partial(_cumsum_kernel, cw=cw),
        grid=(batch, nc),
        in_specs=[pl.BlockSpec((nh, cw), lambda b, j: (0, b * nc + j)),
                  pl.BlockSpec((nh, LANES), lambda b, j: (0, 0)),
                  pl.BlockSpec((cw, cw), lambda b, j: (0, 0))],
        out_specs=[pl.BlockSpec((nh, cw), lambda b, j: (0, b * nc + j)),
                   pl.BlockSpec((nh, LANES), lambda b, j: (0, 0))],
        out_shape=[jax.ShapeDtypeStruct(lf.shape, F32),
                   jax.ShapeDtypeStruct((nh, LANES), F32)],
        scratch_shapes=[pltpu.VMEM((nh, 1), F32)],
        compiler_params=_cparams("arbitrary", "arbitrary"),
        name="forget_cumsum",
    )(lf, lfm_pad, tri)


def _attn_kernel(q_ref, k_ref, v_ref, km_ref, vm_ref, c_ref, cm_ref, o_ref, *, tq):
    qi = pl.program_id(2)
    q = q_ref[0]
    q0 = pl.multiple_of(qi * tq, tq)
    c0 = c_ref[0, :, pl.ds(q0, LANES)][:, 0:1]

    def block(k, v, ck, carry, causal):
        s = _dot_nt(q, k) + (c0 - ck)
        if causal:
            row = lax.broadcasted_iota(jnp.int32, s.shape, 0)
            col = lax.broadcasted_iota(jnp.int32, s.shape, 1)
            s = jnp.where(col <= row, s, -jnp.inf)
        m_new = jnp.max(s, axis=1, keepdims=True)
        if carry is None:
            p = jnp.exp(s - m_new)
            return m_new, jnp.sum(p, axis=1, keepdims=True), _dot(p.astype(BF16), v)
        m, l, acc = carry
        m_new = jnp.maximum(m, m_new)
        a = jnp.exp(m - m_new)
        p = jnp.exp(s - m_new)
        return (m_new, a * l + jnp.sum(p, axis=1, keepdims=True),
                a * acc + _dot(p.astype(BF16), v))

    carry = block(km_ref[0], vm_ref[0], cm_ref[0], None, False)

    def body(j, carry):
        k0 = pl.multiple_of(j * tq, tq)
        return block(k_ref[0, pl.ds(k0, tq), :], v_ref[0, pl.ds(k0, tq), :],
                     c_ref[0, :, pl.ds(k0, tq)], carry, False)

    carry = lax.fori_loop(0, qi, body, carry)
    _, l, acc = block(k_ref[0, pl.ds(q0, tq), :], v_ref[0, pl.ds(q0, tq), :],
                      c_ref[0, :, pl.ds(q0, tq)], carry, True)
    o_ref[...] = (acc / l).astype(o_ref.dtype)


def _attention(qkv, qkv_m, c, cm, batch, seq, tq):
    nq = seq // tq
    hh = N_HEADS
    return pl.pallas_call(
        functools.partial(_attn_kernel, tq=tq),
        grid=(batch, hh, nq),
        in_specs=[pl.BlockSpec((1, tq, HEAD_DIM), lambda b, h, i: (h, b * nq + i, 0)),
                  pl.BlockSpec((1, seq, HEAD_DIM), lambda b, h, i: (hh + h, b, 0)),
                  pl.BlockSpec((1, seq, HEAD_DIM), lambda b, h, i: (2 * hh + h, b, 0)),
                  pl.BlockSpec((1, LANES, HEAD_DIM), lambda b, h, i: (hh + h, 0, 0)),
                  pl.BlockSpec((1, LANES, HEAD_DIM), lambda b, h, i: (2 * hh + h, 0, 0)),
                  pl.BlockSpec((1, 1, seq), lambda b, h, i: (h, 0, b)),
                  pl.BlockSpec((1, 1, LANES), lambda b, h, i: (h, 0, 0))],
        out_specs=pl.BlockSpec((tq, HEAD_DIM), lambda b, h, i: (b * nq + i, h)),
        out_shape=jax.ShapeDtypeStruct((batch * seq, hh * HEAD_DIM), BF16),
        compiler_params=_cparams("parallel", "parallel", "arbitrary"),
        name="fox_attention",
    )(qkv, qkv, qkv, qkv_m, qkv_m, c, cm)


def _conv_kernel(cur_ref, prev_ref, um_ref, w_ref, db_ref, g_ref, b_ref, wo_ref, o_ref,
                 win_ref, h_ref, *, ts, rc, cw):
    i = pl.program_id(1)
    d = cur_ref.shape[1]

    @pl.when(i == 0)
    def _():
        win_ref[0:HALO - N_META, :] = jnp.zeros((HALO - N_META, d), F32)
        win_ref[HALO - N_META:HALO, :] = um_ref[...]

    @pl.when(i > 0)
    def _():
        win_ref[0:HALO, :] = prev_ref[...]

    win_ref[HALO:HALO + ts, :] = cur_ref[...]

    off = HALO - (CONV_TAPS - 1)
    for r in range(ts // rc):
        for cc in range(d // cw):
            cs = slice(cc * cw, (cc + 1) * cw)
            acc = jnp.zeros((rc, cw), F32)
            for k in range(CONV_TAPS):
                acc = acc + w_ref[k:k + 1, cs] * win_ref[r * rc + off + k:r * rc + off + k + rc, cs]
            h_ref[r * rc:(r + 1) * rc, cs] = acc + db_ref[:, cs]

    hn = _ln(h_ref[...], g_ref[...], b_ref[...])
    hs = hn * _sigmoid(hn)
    o_ref[...] = _dot(hs.astype(BF16), wo_ref[...]).astype(o_ref.dtype)


def _conv_module(u, u_m, dw_w, dw_b, ln_g, ln_b, w_o, batch, seq, ts):
    d = u.shape[1]
    ns = seq // ts
    hb = ts // HALO
    return pl.pallas_call(
        functools.partial(_conv_kernel, ts=ts, rc=16, cw=512),
        grid=(batch, ns),
        in_specs=[pl.BlockSpec((ts, d), lambda b, i: (b * ns + i, 0)),
                  pl.BlockSpec((HALO, d), lambda b, i: (jnp.maximum((b * ns + i) * hb - 1, 0), 0)),
                  pl.BlockSpec((N_META, d), lambda b, i: (0, 0)),
                  pl.BlockSpec((CONV_TAPS, d), lambda b, i: (0, 0)),
                  pl.BlockSpec((1, d), lambda b, i: (0, 0)),
                  pl.BlockSpec((1, d), lambda b, i: (0, 0)),
                  pl.BlockSpec((1, d), lambda b, i: (0, 0)),
                  pl.BlockSpec((d, d), lambda b, i: (0, 0))],
        out_specs=pl.BlockSpec((ts, d), lambda b, i: (b * ns + i, 0)),
        out_shape=jax.ShapeDtypeStruct((batch * seq, d), BF16),
        scratch_shapes=[pltpu.VMEM((HALO + ts, d), F32), pltpu.VMEM((ts, d), F32)],
        compiler_params=_cparams("parallel", "arbitrary"),
        name="conv_module",
    )(u, u, u_m, dw_w, dw_b, ln_g, ln_b, w_o)


def _mix_kernel(yc_ref, ya_ref, gc_ref, ga_ref, x_ref, g0_ref, b0_ref, wo_ref, g1_ref, b1_ref,
                h_ref, hb_ref):
    m = (gc_ref[...].astype(F32) * yc_ref[...].astype(F32)
         + ga_ref[...].astype(F32) * ya_ref[...].astype(F32))
    mix = _dot(m.astype(BF16), wo_ref[...])
    h0 = _ln(x_ref[...], g0_ref[...], b0_ref[...])
    h1 = _ln(DN_ALPHA * h0 + mix, g1_ref[...], b1_ref[...])
    h_ref[...] = h1
    hb_ref[...] = h1.astype(BF16)


def _mix(yc, ya, gates, x, g0, b0, w_out, g1, b1, tm):
    m, d = x.shape
    row = lambda i: (i, 0)
    vec = pl.BlockSpec((1, d), lambda i: (0, 0))
    return pl.pallas_call(
        _mix_kernel,
        grid=(m // tm,),
        in_specs=[pl.BlockSpec((tm, d), row), pl.BlockSpec((tm, d), row),
                  pl.BlockSpec((tm, d), row), pl.BlockSpec((tm, d), lambda i: (i, 1)),
                  pl.BlockSpec((tm, d), row), vec, vec,
                  pl.BlockSpec((d, d), lambda i: (0, 0)), vec, vec],
        out_specs=[pl.BlockSpec((tm, d), row), pl.BlockSpec((tm, d), row)],
        out_shape=[jax.ShapeDtypeStruct((m, d), F32), jax.ShapeDtypeStruct((m, d), BF16)],
        compiler_params=_cparams("parallel"),
        name="merge_ln1",
    )(yc, ya, gates, gates, x, g0, b0, w_out, g1, b1)


def _kth_largest_rows(work, k, out_ref=None):
    m = None
    for t in range(k):
        m = jnp.max(work, axis=0, keepdims=True)
        if out_ref is not None:
            out_ref[t:t + 1, :] = m
        if t + 1 < k:
            work = jnp.where(work == m, -jnp.inf, work)
    return m


def _peer_kernel(qp_ref, sk_ref, hb_ref, h_ref, u_ref, vt_ref, g2_ref, b2_ref, o_ref,
                 s0_ref, s1_ref, e1_ref, f0_ref, tau_ref, sv0_ref, sv1_ref, cand_ref,
                 hc_ref, acc_ref, *, ig):
    j = pl.program_id(1)
    nk = PEER_NKEYS

    @pl.when(j == 0)
    def _():
        acc_ref[...] = jnp.zeros_like(acc_ref)
        for h in range(PEER_HEADS):
            q0 = qp_ref[:, (2 * h) * nk:(2 * h + 1) * nk]
            q1 = qp_ref[:, (2 * h + 1) * nk:(2 * h + 2) * nk]
            s0 = _dot_nt(sk_ref[h, 0], q0)
            s1 = _dot_nt(sk_ref[h, 1], q1)
            _kth_largest_rows(s0, PEER_TOPK, sv0_ref)
            _kth_largest_rows(s1, PEER_TOPK, sv1_ref)
            sv1 = sv1_ref[...]
            for a in range(PEER_TOPK):
                cand_ref[a * PEER_TOPK:(a + 1) * PEER_TOPK, :] = sv0_ref[a:a + 1, :] + sv1
            cand = cand_ref[...]
            tau = _kth_largest_rows(cand, PEER_TOPK)
            m0 = sv0_ref[0:1, :]
            m1 = sv1_ref[0:1, :]
            z = jnp.sum(jnp.where(cand >= tau, jnp.exp(cand - (m0 + m1)), 0.0), axis=0, keepdims=True)
            s0_ref[h] = s0
            s1_ref[h] = s1
            e1_ref[h] = jnp.exp(s1 - m1)
            f0_ref[h] = jnp.exp(s0 - m0) / z
            tau_ref[h:h + 1, :] = tau

    act = _dot_nt(u_ref[...], hb_ref[...])
    for ii in range(ig):
        i = j * ig + ii
        w = None
        for h in range(PEER_HEADS):
            val = s0_ref[h, pl.ds(i, 1), :] + s1_ref[h]
            wh = jnp.where(val >= tau_ref[h:h + 1, :], e1_ref[h], 0.0) * f0_ref[h, pl.ds(i, 1), :]
            w = wh if w is None else w + wh
        a = act[ii * nk:(ii + 1) * nk, :]
        gelu = 0.5 * a * (1.0 + lax.erf(a * (1.0 / math.sqrt(2.0))))
        hc_ref[ii * nk:(ii + 1) * nk, :] = (w * gelu).astype(BF16)
    acc_ref[...] += _dot(vt_ref[...], hc_ref[...])

    @pl.when(j == pl.num_programs(1) - 1)
    def _():
        y = DN_ALPHA * h_ref[...] + acc_ref[...].T
        o_ref[...] = _ln(y, g2_ref[...], b2_ref[...])


def _peer(qp, sk, hb, h, u, vt, g2, b2, tn, ec):
    m, d = h.shape
    ne = u.shape[0]
    ig = ec // PEER_NKEYS
    row = lambda i, j: (i, 0)
    big = lambda: pltpu.VMEM((PEER_HEADS, PEER_NKEYS, tn), F32)
    return pl.pallas_call(
        functools.partial(_peer_kernel, ig=ig),
        grid=(m // tn, ne // ec),
        in_specs=[pl.BlockSpec((tn, qp.shape[1]), row),
                  pl.BlockSpec(sk.shape, lambda i, j: (0, 0, 0, 0)),
                  pl.BlockSpec((tn, d), row),
                  pl.BlockSpec((tn, d), row),
                  pl.BlockSpec((ec, d), lambda i, j: (j, 0)),
                  pl.BlockSpec((d, ec), lambda i, j: (0, j)),
                  pl.BlockSpec((1, d), lambda i, j: (0, 0)),
                  pl.BlockSpec((1, d), lambda i, j: (0, 0))],
        out_specs=pl.BlockSpec((tn, d), row),
        out_shape=jax.ShapeDtypeStruct((m, d), F32),
        scratch_shapes=[big(), big(), big(), big(),
                        pltpu.VMEM((PEER_HEADS, tn), F32),
                        pltpu.VMEM((PEER_TOPK, tn), F32),
                        pltpu.VMEM((PEER_TOPK, tn), F32),
                        pltpu.VMEM((PEER_TOPK * PEER_TOPK, tn), F32),
                        pltpu.VMEM((ec, tn), BF16),
                        pltpu.VMEM((d, tn), F32)],
        compiler_params=_cparams("parallel", "arbitrary"),
        name="peer_dense",
    )(qp, sk, hb, h, u, vt, g2, b2)


def kernel(x, meta_tokens, ln_in_g, ln_in_b, w_in, b_in, conv_dw_w, conv_dw_b, conv_ln_g, conv_ln_b, w_conv_out, w_attn_out, w_out, ln1_g, ln1_b, peer_w_q, peer_subkeys, peer_u, peer_v, ln2_g, ln2_b):
    batch, seq, d = x.shape
    assert w_in.shape[0] == 1, "one layer only"
    assert d == N_HEADS * HEAD_DIM
    t = batch * seq
    off_f = 5 * d
    off_gate = off_f + N_HEADS
    row = lambda v: v.reshape(1, -1)

    w_main = w_in[0][:, :off_f].astype(BF16)
    b_main = row(b_in[0][:off_f])
    w_f_t = w_in[0][:, off_f:off_gate].T.astype(BF16)
    b_f = b_in[0][off_f:off_gate].reshape(N_HEADS, 1)
    w_gate = w_in[0][:, off_gate:].astype(BF16)
    b_gate = row(b_in[0][off_gate:])
    qkv_scale = jnp.concatenate([jnp.full((1, d), HEAD_DIM ** -0.5, F32), jnp.ones((1, 2 * d), F32)], axis=1)
    zeros_d = jnp.zeros((1, d), F32)

    x2 = x.reshape(t, d)
    g0, b0 = row(ln_in_g), row(ln_in_b)
    tm = min(1024, t)

    h0 = _ln_cast(x2, g0, b0, 512)
    h0_m = _ln_cast(meta_tokens, g0, b0, N_META)
    u = _mm_glu(h0, w_main, b_main, d, tm, 512)
    u_m = _mm_glu(h0_m, w_main, b_main, d, N_META, 512)
    qkv = _mm_heads(h0, w_main, b_main, qkv_scale, 2 * d, 3 * d, tm, 512)
    qkv_m = _mm_heads(h0_m, w_main, b_main, qkv_scale, 2 * d, 3 * d, N_META, 512)
    gates = _mm_act(h0, w_gate, b_gate, "sigmoid", tm, 512, "proj_gates")
    logf = _forget(w_f_t, h0, b_f, 512)
    logf_m = _forget(w_f_t, h0_m, b_f, N_META)

    logf_m_pad = jnp.pad(logf_m, ((0, 0), (0, LANES - N_META)))
    c, c_m = _cumsum(logf, logf_m_pad, batch, seq, 512)
    c_m = jnp.where(jnp.arange(LANES)[None, :] < N_META, c_m, jnp.inf)
    qkv_m = jnp.pad(qkv_m, ((0, 0), (0, LANES - N_META), (0, 0)))

    o_attn = _attention(qkv, qkv_m, c.reshape(N_HEADS, 1, t), c_m.reshape(N_HEADS, 1, LANES),
                        batch, seq, min(512, seq))
    y_attn = _mm_act(o_attn, w_attn_out[0].astype(BF16), zeros_d, "none", tm, 512, "attn_out")
    y_conv = _conv_module(u, u_m, conv_dw_w[0], row(conv_dw_b[0]), row(conv_ln_g[0]), row(conv_ln_b[0]),
                          w_conv_out[0].astype(BF16), batch, seq, 256)

    h1, h1b = _mix(y_conv, y_attn, gates, x2, g0, b0, w_out[0].astype(BF16),
                   row(ln1_g[0]), row(ln1_b[0]), 512)

    qp = _mm_act(h1b, peer_w_q[0].astype(BF16), zeros_d, "none", tm, 512, "peer_query")
    out = _peer(qp, peer_subkeys[0].astype(BF16), h1b, h1, peer_u[0].astype(BF16),
                peer_v[0].T.astype(BF16), row(ln2_g[0]), row(ln2_b[0]), 512, 512)
    return out.reshape(batch, seq, d)
```

```python
import functools
import math

import jax
import jax.numpy as jnp
from jax import lax
from jax.experimental import pallas as pl
from jax.experimental.pallas import tpu as pltpu

F32 = jnp.float32
BF16 = jnp.bfloat16

LN_EPS = 1e-5
N_META = 16
CONV_TAPS = 31
N_HEADS = 16
HEAD_DIM = 128
PEER_HEADS = 8
PEER_NKEYS = 128
PEER_TOPK = 16
DN_ALPHA = 2.0 ** 0.25
LANES = 128
SUBLANES = 8
LOG2E = math.log2(math.e)
HALO = 32
VMEM_LIMIT = 56 * 1024 * 1024


def _cparams(*sem):
    return pltpu.CompilerParams(dimension_semantics=sem, vmem_limit_bytes=VMEM_LIMIT)


def _ln(x, g, b):
    mu = jnp.mean(x, axis=-1, keepdims=True)
    xc = x - mu
    var = jnp.mean(xc * xc, axis=-1, keepdims=True)
    return xc * lax.rsqrt(var + LN_EPS) * g + b


def _sigmoid(x):
    return 1.0 / (1.0 + jnp.exp(-x))


def _dot(a, b):
    return jnp.dot(a, b, preferred_element_type=F32)


def _dot_nt(a, b):
    return lax.dot_general(a, b, (((1,), (1,)), ((), ())), preferred_element_type=F32)


def _ln_cast_kernel(x_ref, g_ref, b_ref, o_ref):
    o_ref[...] = _ln(x_ref[...], g_ref[...], b_ref[...]).astype(o_ref.dtype)


def _ln_cast(x, g, b, tm):
    m, d = x.shape
    return pl.pallas_call(
        _ln_cast_kernel,
        grid=(m // tm,),
        in_specs=[pl.BlockSpec((tm, d), lambda i: (i, 0)),
                  pl.BlockSpec((1, d), lambda i: (0, 0)),
                  pl.BlockSpec((1, d), lambda i: (0, 0))],
        out_specs=pl.BlockSpec((tm, d), lambda i: (i, 0)),
        out_shape=jax.ShapeDtypeStruct((m, d), BF16),
        compiler_params=_cparams("parallel"),
        name="ln_in",
    )(x, g, b)


def _mm_glu_kernel(x_ref, wa_ref, wg_ref, ba_ref, bg_ref, o_ref):
    x = x_ref[...]
    a = _dot(x, wa_ref[...]) + ba_ref[...]
    g = _dot(x, wg_ref[...]) + bg_ref[...]
    o_ref[...] = (a * _sigmoid(g)).astype(o_ref.dtype)


def _mm_glu(x, w, b, n, tm, tn):
    m, d = x.shape
    nb = n // tn
    return pl.pallas_call(
        _mm_glu_kernel,
        grid=(m // tm, nb),
        in_specs=[pl.BlockSpec((tm, d), lambda i, j: (i, 0)),
                  pl.BlockSpec((d, tn), lambda i, j: (0, j)),
                  pl.BlockSpec((d, tn), lambda i, j: (0, nb + j)),
                  pl.BlockSpec((1, tn), lambda i, j: (0, j)),
                  pl.BlockSpec((1, tn), lambda i, j: (0, nb + j))],
        out_specs=pl.BlockSpec((tm, tn), lambda i, j: (i, j)),
        out_shape=jax.ShapeDtypeStruct((m, n), F32),
        compiler_params=_cparams("parallel", "arbitrary"),
        name="proj_glu",
    )(x, w, w, b, b)


def _mm_heads_kernel(x_ref, w_ref, b_ref, s_ref, o_ref, *, nh):
    acc = (_dot(x_ref[...], w_ref[...]) + b_ref[...]) * s_ref[...]
    for hh in range(nh):
        o_ref[hh] = acc[:, hh * HEAD_DIM:(hh + 1) * HEAD_DIM].astype(o_ref.dtype)


def _mm_heads(x, w, b, scale, col0, n, tm, tn):
    m, d = x.shape
    nh = tn // HEAD_DIM
    cb = col0 // tn
    return pl.pallas_call(
        functools.partial(_mm_heads_kernel, nh=nh),
        grid=(m // tm, n // tn),
        in_specs=[pl.BlockSpec((tm, d), lambda i, j: (i, 0)),
                  pl.BlockSpec((d, tn), lambda i, j: (0, cb + j)),
                  pl.BlockSpec((1, tn), lambda i, j: (0, cb + j)),
                  pl.BlockSpec((1, tn), lambda i, j: (0, j))],
        out_specs=pl.BlockSpec((nh, tm, HEAD_DIM), lambda i, j: (j, i, 0)),
        out_shape=jax.ShapeDtypeStruct((n // HEAD_DIM, m, HEAD_DIM), BF16),
        compiler_params=_cparams("parallel", "arbitrary"),
        name="proj_qkv",
    )(x, w, b, scale)


def _mm_act_kernel(x_ref, w_ref, b_ref, o_ref, *, act):
    y = _dot(x_ref[...], w_ref[...]) + b_ref[...]
    if act == "sigmoid":
        y = _sigmoid(y)
    o_ref[...] = y.astype(o_ref.dtype)


def _mm_act(x, w, b, act, tm, tn, name):
    m, d = x.shape
    n = w.shape[1]
    return pl.pallas_call(
        functools.partial(_mm_act_kernel, act=act),
        grid=(m // tm, n // tn),
        in_specs=[pl.BlockSpec((tm, d), lambda i, j: (i, 0)),
                  pl.BlockSpec((d, tn), lambda i, j: (0, j)),
                  pl.BlockSpec((1, tn), lambda i, j: (0, j))],
        out_specs=pl.BlockSpec((tm, tn), lambda i, j: (i, j)),
        out_shape=jax.ShapeDtypeStruct((m, n), BF16),
        compiler_params=_cparams("parallel", "arbitrary"),
        name=name,
    )(x, w, b)


def _forget_kernel(w_ref, x_ref, b_ref, o_ref):
    z = _dot_nt(w_ref[...], x_ref[...]) + b_ref[...]
    o_ref[...] = jnp.minimum(z, 0.0) - jnp.log1p(jnp.exp(-jnp.abs(z)))


def _forget(wt, x, b, tm):
    m, d = x.shape
    nh = wt.shape[0]
    return pl.pallas_call(
        _forget_kernel,
        grid=(m // tm,),
        in_specs=[pl.BlockSpec((nh, d), lambda i: (0, 0)),
                  pl.BlockSpec((tm, d), lambda i: (i, 0)),
                  pl.BlockSpec((nh, 1), lambda i: (0, 0))],
        out_specs=pl.BlockSpec((nh, tm), lambda i: (0, i)),
        out_shape=jax.ShapeDtypeStruct((nh, m), F32),
        compiler_params=_cparams("parallel"),
        name="forget_logits",
    )(wt, x, b)


def _split_dot(x, tri):
    hi = x.astype(BF16)
    r = x - hi.astype(F32)
    mid = r.astype(BF16)
    lo = (r - mid.astype(F32)).astype(BF16)
    return _dot(hi, tri) + _dot(mid, tri) + _dot(lo, tri)


def _cumsum_kernel(lf_ref, lfm_ref, tri_ref, c_ref, cm_ref, carry_ref, *, cw):
    j = pl.program_id(1)
    cm = _split_dot(lfm_ref[...], tri_ref[0:LANES, 0:LANES])

    @pl.when(j == 0)
    def _():
        carry_ref[...] = cm[:, N_META - 1:N_META]

    c = _split_dot(lf_ref[...], tri_ref[...]) + carry_ref[...]
    c_ref[...] = c
    cm_ref[...] = cm
    carry_ref[...] = c[:, cw - 1:cw]


def _cumsum(lf, lfm_pad, batch, seq, cw):
    nh = lf.shape[0]
    nc = seq // cw
    tri = (lax.broadcasted_iota(jnp.int32, (cw, cw), 0)
           <= lax.broadcasted_iota(jnp.int32, (cw, cw), 1)).astype(BF16)
    return pl.pallas_call(
        functools.partial(_cumsum_kernel, cw=cw),
        grid=(batch, nc),
        in_specs=[pl.BlockSpec((nh, cw), lambda b, j: (0, b * nc + j)),
                  pl.BlockSpec((nh, LANES), lambda b, j: (0, 0)),
                  pl.BlockSpec((cw, cw), lambda b, j: (0, 0))],
        out_specs=[pl.BlockSpec((nh, cw), lambda b, j: (0, b * nc + j)),
                   pl.BlockSpec((nh, LANES), lambda b, j: (0, 0))],
        out_shape=[jax.ShapeDtypeStruct(lf.shape, F32),
                   jax.ShapeDtypeStruct((nh, LANES), F32)],
        scratch_shapes=[pltpu.VMEM((nh, 1), F32)],
        compiler_params=_cparams("arbitrary", "arbitrary"),
        name="forget_cumsum",
    )(lf, lfm_pad, tri)


def _attn_kernel(q_ref, k_ref, v_ref, km_ref, vm_ref, c_ref, cm_ref, o_ref, *, tq, nsub):
    qi = pl.program_id(2)
    hq = tq // nsub
    q0 = pl.multiple_of(qi * tq, tq)
    c0 = c_ref[0, :, pl.ds(q0, LANES)][:, 0:1]
    qs = [q_ref[0, u * hq:(u + 1) * hq, :] for u in range(nsub)]

    def update(q, k, v, bias, carry, row0=None):
        s = _dot_nt(q, k) + bias
        if row0 is not None:
            row = lax.broadcasted_iota(jnp.int32, s.shape, 0) + row0
            col = lax.broadcasted_iota(jnp.int32, s.shape, 1)
            s = jnp.where(col <= row, s, -jnp.inf)
        m_new = jnp.max(s, axis=1, keepdims=True)
        if carry is None:
            p = jnp.exp2(s - m_new)
            return m_new, jnp.sum(p, axis=1, keepdims=True), _dot(p.astype(BF16), v)
        m, l, acc = carry
        m_new = jnp.maximum(m, m_new)
        a = jnp.exp2(m - m_new)
        p = jnp.exp2(s - m_new)
        return (m_new, a * l + jnp.sum(p, axis=1, keepdims=True),
                a * acc + _dot(p.astype(BF16), v))

    bias_m = (c0 - cm_ref[0]) * LOG2E
    carries = tuple(update(q, km_ref[0], vm_ref[0], bias_m, None) for q in qs)

    def body(j, carries):
        k0 = pl.multiple_of(j * tq, tq)
        k = k_ref[0, pl.ds(k0, tq), :]
        v = v_ref[0, pl.ds(k0, tq), :]
        bias = (c0 - c_ref[0, :, pl.ds(k0, tq)]) * LOG2E
        return tuple(update(q, k, v, bias, c) for q, c in zip(qs, carries))

    carries = lax.fori_loop(0, qi, body, carries)

    for u in range(nsub):
        nkeys = (u + 1) * hq
        bias = (c0 - c_ref[0, :, pl.ds(q0, nkeys)]) * LOG2E
        _, l, acc = update(qs[u], k_ref[0, pl.ds(q0, nkeys), :], v_ref[0, pl.ds(q0, nkeys), :],
                           bias, carries[u], row0=u * hq)
        o_ref[u * hq:(u + 1) * hq, :] = (acc / l).astype(o_ref.dtype)


def _attention(qkv, qkv_m, c, cm, batch, seq, tq):
    nq = seq // tq
    hh = N_HEADS
    return pl.pallas_call(
        functools.partial(_attn_kernel, tq=tq, nsub=1),
        grid=(batch, hh, nq),
        in_specs=[pl.BlockSpec((1, tq, HEAD_DIM), lambda b, h, i: (h, b * nq + i, 0)),
                  pl.BlockSpec((1, seq, HEAD_DIM), lambda b, h, i: (hh + h, b, 0)),
                  pl.BlockSpec((1, seq, HEAD_DIM), lambda b, h, i: (2 * hh + h, b, 0)),
                  pl.BlockSpec((1, LANES, HEAD_DIM), lambda b, h, i: (hh + h, 0, 0)),
                  pl.BlockSpec((1, LANES, HEAD_DIM), lambda b, h, i: (2 * hh + h, 0, 0)),
                  pl.BlockSpec((1, 1, seq), lambda b, h, i: (h, 0, b)),
                  pl.BlockSpec((1, 1, LANES), lambda b, h, i: (h, 0, 0))],
        out_specs=pl.BlockSpec((tq, HEAD_DIM), lambda b, h, i: (b * nq + i, h)),
        out_shape=jax.ShapeDtypeStruct((batch * seq, hh * HEAD_DIM), BF16),
        compiler_params=_cparams("parallel", "parallel", "arbitrary"),
        name="fox_attention",
    )(qkv, qkv, qkv, qkv_m, qkv_m, c, cm)


def _conv_kernel(cur_ref, prev_ref, um_ref, w_ref, db_ref, g_ref, b_ref, wo_ref, o_ref,
                 win_ref, h_ref, *, ts, rc, cw):
    i = pl.program_id(1)
    d = cur_ref.shape[1]

    @pl.when(i == 0)
    def _():
        win_ref[0:HALO - N_META, :] = jnp.zeros((HALO - N_META, d), F32)
        win_ref[HALO - N_META:HALO, :] = um_ref[...]

    @pl.when(i > 0)
    def _():
        win_ref[0:HALO, :] = prev_ref[...]

    win_ref[HALO:HALO + ts, :] = cur_ref[...]

    off = HALO - (CONV_TAPS - 1)
    nwin = rc + HALO

    def row_chunk(r, _):
        r0 = pl.multiple_of(r * rc, rc)
        for cc in range(d // cw):
            cs = slice(cc * cw, (cc + 1) * cw)
            wd = win_ref[pl.ds(r0, nwin), cs]
            acc = jnp.zeros((rc, cw), F32)
            for ph in range(SUBLANES):
                ws = wd if ph == 0 else pltpu.roll(wd, nwin - ph, axis=0)
                for k in range(CONV_TAPS):
                    if (off + k) % SUBLANES == ph:
                        a0 = off + k - ph
                        acc = acc + w_ref[k:k + 1, cs] * ws[a0:a0 + rc]
            h_ref[pl.ds(r0, rc), cs] = acc + db_ref[:, cs]
        return 0

    lax.fori_loop(0, ts // rc, row_chunk, 0)

    hn = _ln(h_ref[...], g_ref[...], b_ref[...])
    hs = hn * _sigmoid(hn)
    o_ref[...] = _dot(hs.astype(BF16), wo_ref[...]).astype(o_ref.dtype)


def _conv_module(u, u_m, dw_w, dw_b, ln_g, ln_b, w_o, batch, seq, ts):
    d = u.shape[1]
    ns = seq // ts
    hb = ts // HALO
    return pl.pallas_call(
        functools.partial(_conv_kernel, ts=ts, rc=32, cw=256),
        grid=(batch, ns),
        in_specs=[pl.BlockSpec((ts, d), lambda b, i: (b * ns + i, 0)),
                  pl.BlockSpec((HALO, d), lambda b, i: (jnp.maximum((b * ns + i) * hb - 1, 0), 0)),
                  pl.BlockSpec((N_META, d), lambda b, i: (0, 0)),
                  pl.BlockSpec((CONV_TAPS, d), lambda b, i: (0, 0)),
                  pl.BlockSpec((1, d), lambda b, i: (0, 0)),
                  pl.BlockSpec((1, d), lambda b, i: (0, 0)),
                  pl.BlockSpec((1, d), lambda b, i: (0, 0)),
                  pl.BlockSpec((d, d), lambda b, i: (0, 0))],
        out_specs=pl.BlockSpec((ts, d), lambda b, i: (b * ns + i, 0)),
        out_shape=jax.ShapeDtypeStruct((batch * seq, d), BF16),
        scratch_shapes=[pltpu.VMEM((HALO + ts, d), F32), pltpu.VMEM((ts, d), F32)],
        compiler_params=_cparams("parallel", "arbitrary"),
        name="conv_module",
    )(u, u, u_m, dw_w, dw_b, ln_g, ln_b, w_o)


def _mix_kernel(yc_ref, ya_ref, gc_ref, ga_ref, x_ref, g0_ref, b0_ref, wo_ref, g1_ref, b1_ref,
                h_ref, hb_ref):
    m = (gc_ref[...].astype(F32) * yc_ref[...].astype(F32)
         + ga_ref[...].astype(F32) * ya_ref[...].astype(F32))
    mix = _dot(m.astype(BF16), wo_ref[...])
    h0 = _ln(x_ref[...], g0_ref[...], b0_ref[...])
    h1 = _ln(DN_ALPHA * h0 + mix, g1_ref[...], b1_ref[...])
    h_ref[...] = h1
    hb_ref[...] = h1.astype(BF16)


def _mix(yc, ya, gates, x, g0, b0, w_out, g1, b1, tm):
    m, d = x.shape
    row = lambda i: (i, 0)
    vec = pl.BlockSpec((1, d), lambda i: (0, 0))
    return pl.pallas_call(
        _mix_kernel,
        grid=(m // tm,),
        in_specs=[pl.BlockSpec((tm, d), row), pl.BlockSpec((tm, d), row),
                  pl.BlockSpec((tm, d), row), pl.BlockSpec((tm, d), lambda i: (i, 1)),
                  pl.BlockSpec((tm, d), row), vec, vec,
                  pl.BlockSpec((d, d), lambda i: (0, 0)), vec, vec],
        out_specs=[pl.BlockSpec((tm, d), row), pl.BlockSpec((tm, d), row)],
        out_shape=[jax.ShapeDtypeStruct((m, d), F32), jax.ShapeDtypeStruct((m, d), BF16)],
        compiler_params=_cparams("parallel"),
        name="merge_ln1",
    )(yc, ya, gates, gates, x, g0, b0, w_out, g1, b1)


def _kth_largest_rows(work, k, out_ref=None):
    m = None
    for t in range(k):
        m = jnp.max(work, axis=0, keepdims=True)
        if out_ref is not None:
            out_ref[t:t + 1, :] = m
        if t + 1 < k:
            work = jnp.where(work == m, -jnp.inf, work)
    return m


def _peer_kernel(qp_ref, sk_ref, hb_ref, h_ref, u_ref, vt_ref, g2_ref, b2_ref, o_ref,
                 s0_ref, s1_ref, e1_ref, f0_ref, tau_ref, sv0_ref, sv1_ref, cand_ref,
                 act0_ref, act1_ref, hc0_ref, hc1_ref, acc_ref, *, ig, nj):
    acts = (act0_ref, act1_ref)
    hcs = (hc0_ref, hc1_ref)
    j = pl.program_id(1)
    nk = PEER_NKEYS

    @pl.when(j == 0)
    def _():
        acc_ref[...] = jnp.zeros_like(acc_ref)
        act1_ref[...] = jnp.zeros_like(act1_ref)
        hc0_ref[...] = jnp.zeros_like(hc0_ref)
        for h in range(PEER_HEADS):
            q0 = qp_ref[:, (2 * h) * nk:(2 * h + 1) * nk]
            q1 = qp_ref[:, (2 * h + 1) * nk:(2 * h + 2) * nk]
            s0 = _dot_nt(sk_ref[h, 0], q0)
            s1 = _dot_nt(sk_ref[h, 1], q1)
            _kth_largest_rows(s0, PEER_TOPK, sv0_ref)
            _kth_largest_rows(s1, PEER_TOPK, sv1_ref)
            lo8 = sv1_ref[0:8, :]
            cand_ref[0:16, :] = sv0_ref[0:1, :] + sv1_ref[...]
            cand_ref[16:24, :] = sv0_ref[1:2, :] + lo8
            cand_ref[24:32, :] = sv0_ref[2:3, :] + lo8
            cand_ref[32:40, :] = sv0_ref[3:4, :] + lo8
            cand_ref[40:48, :] = sv0_ref[8:16, :] + sv1_ref[0:1, :]
            for b in range(3):
                cand_ref[48 + 4 * b:52 + 4 * b, :] = sv0_ref[4:8, :] + sv1_ref[b:b + 1, :]
            cand_ref[60:64, :] = jnp.full((4, cand_ref.shape[1]), -jnp.inf, F32)
            cand = cand_ref[...]
            tau = _kth_largest_rows(cand, PEER_TOPK)
            m0 = sv0_ref[0:1, :]
            m1 = sv1_ref[0:1, :]
            z = jnp.sum(jnp.where(cand >= tau, jnp.exp(cand - (m0 + m1)), 0.0), axis=0, keepdims=True)
            s0_ref[h] = s0
            s1_ref[h] = s1
            e1_ref[h] = jnp.exp(s1 - m1)
            f0_ref[h] = jnp.exp(s0 - m0) / z
            tau_ref[h:h + 1, :] = tau

    def step(par):
        base = jnp.clip(j - 1, 0, nj - 1) * ig
        tw = acc_ref.shape[1] // 2

        def score_piece(ts):
            acts[par][:, ts] = _dot_nt(u_ref[...], hb_ref[ts, :])

        def gate_piece(ts, ii):
            i = base + ii
            w = None
            for h in range(PEER_HEADS):
                val = s0_ref[h, pl.ds(i, 1), ts] + s1_ref[h, :, ts]
                wh = (jnp.where(val >= tau_ref[h:h + 1, ts], e1_ref[h, :, ts], 0.0)
                      * f0_ref[h, pl.ds(i, 1), ts])
                w = wh if w is None else w + wh
            a = acts[1 - par][ii * nk:(ii + 1) * nk, ts]
            gelu = 0.5 * a * (1.0 + lax.erf(a * (1.0 / math.sqrt(2.0))))
            hcs[1 - par][ii * nk:(ii + 1) * nk, ts] = (w * gelu).astype(BF16)

        def value_piece(ts):
            acc_ref[:, ts] += _dot(vt_ref[...], hcs[par][:, ts])

        for th in range(2):
            ts = slice(th * tw, (th + 1) * tw)
            score_piece(ts)
            for ii in range(0, ig // 2):
                gate_piece(ts, ii)
            value_piece(ts)
            for ii in range(ig // 2, ig):
                gate_piece(ts, ii)

    @pl.when(j % 2 == 0)
    def _():
        step(0)

    @pl.when(j % 2 == 1)
    def _():
        step(1)

    @pl.when(j == pl.num_programs(1) - 1)
    def _():
        y = DN_ALPHA * h_ref[...] + acc_ref[...].T
        o_ref[...] = _ln(y, g2_ref[...], b2_ref[...])


def _peer(qp, sk, hb, h, u, vt, g2, b2, tn, ec):
    m, d = h.shape
    ne = u.shape[0]
    ig = ec // PEER_NKEYS
    nj = ne // ec
    row = lambda i, j: (i, 0)
    big = lambda: pltpu.VMEM((PEER_HEADS, PEER_NKEYS, tn), F32)
    return pl.pallas_call(
        functools.partial(_peer_kernel, ig=ig, nj=nj),
        grid=(m // tn, nj + 2),
        in_specs=[pl.BlockSpec((tn, qp.shape[1]), row),
                  pl.BlockSpec(sk.shape, lambda i, j: (0, 0, 0, 0)),
                  pl.BlockSpec((tn, d), row),
                  pl.BlockSpec((tn, d), row),
                  pl.BlockSpec((ec, d), lambda i, j: (jnp.minimum(j, nj - 1), 0)),
                  pl.BlockSpec((d, ec), lambda i, j: (0, jnp.clip(j - 2, 0, nj - 1))),
                  pl.BlockSpec((1, d), lambda i, j: (0, 0)),
                  pl.BlockSpec((1, d), lambda i, j: (0, 0))],
        out_specs=pl.BlockSpec((tn, d), row),
        out_shape=jax.ShapeDtypeStruct((m, d), F32),
        scratch_shapes=[big(), big(), big(), big(),
                        pltpu.VMEM((PEER_HEADS, tn), F32),
                        pltpu.VMEM((PEER_TOPK, tn), F32),
                        pltpu.VMEM((PEER_TOPK, tn), F32),
                        pltpu.VMEM((64, tn), F32),
                        pltpu.VMEM((ec, tn), F32), pltpu.VMEM((ec, tn), F32),
                        pltpu.VMEM((ec, tn), BF16), pltpu.VMEM((ec, tn), BF16),
                        pltpu.VMEM((d, tn), F32)],
        compiler_params=_cparams("parallel", "arbitrary"),
        name="peer_dense",
    )(qp, sk, hb, h, u, vt, g2, b2)


def kernel(x, meta_tokens, ln_in_g, ln_in_b, w_in, b_in, conv_dw_w, conv_dw_b, conv_ln_g, conv_ln_b, w_conv_out, w_attn_out, w_out, ln1_g, ln1_b, peer_w_q, peer_subkeys, peer_u, peer_v, ln2_g, ln2_b):
    batch, seq, d = x.shape
    assert w_in.shape[0] == 1, "one layer only"
    assert d == N_HEADS * HEAD_DIM
    t = batch * seq
    off_f = 5 * d
    off_gate = off_f + N_HEADS
    row = lambda v: v.reshape(1, -1)

    w_main = w_in[0][:, :off_f].astype(BF16)
    b_main = row(b_in[0][:off_f])
    w_f_t = w_in[0][:, off_f:off_gate].T.astype(BF16)
    b_f = b_in[0][off_f:off_gate].reshape(N_HEADS, 1)
    w_gate = w_in[0][:, off_gate:].astype(BF16)
    b_gate = row(b_in[0][off_gate:])
    qkv_scale = jnp.concatenate([jnp.full((1, d), LOG2E * HEAD_DIM ** -0.5, F32), jnp.ones((1, 2 * d), F32)], axis=1)
    zeros_d = jnp.zeros((1, d), F32)

    x2 = x.reshape(t, d)
    g0, b0 = row(ln_in_g), row(ln_in_b)
    tm = min(1024, t)

    h0 = _ln_cast(x2, g0, b0, 512)
    h0_m = _ln_cast(meta_tokens, g0, b0, N_META)
    u = _mm_glu(h0, w_main, b_main, d, tm, 512)
    u_m = _mm_glu(h0_m, w_main, b_main, d, N_META, 512)
    qkv = _mm_heads(h0, w_main, b_main, qkv_scale, 2 * d, 3 * d, tm, 512)
    qkv_m = _mm_heads(h0_m, w_main, b_main, qkv_scale, 2 * d, 3 * d, N_META, 512)
    gates = _mm_act(h0, w_gate, b_gate, "sigmoid", tm, 512, "proj_gates")
    logf = _forget(w_f_t, h0, b_f, 512)
    logf_m = _forget(w_f_t, h0_m, b_f, N_META)

    logf_m_pad = jnp.pad(logf_m, ((0, 0), (0, LANES - N_META)))
    c, c_m = _cumsum(logf, logf_m_pad, batch, seq, 512)
    c_m = jnp.where(jnp.arange(LANES)[None, :] < N_META, c_m, jnp.inf)
    qkv_m = jnp.pad(qkv_m, ((0, 0), (0, LANES - N_META), (0, 0)))

    o_attn = _attention(qkv, qkv_m, c.reshape(N_HEADS, 1, t), c_m.reshape(N_HEADS, 1, LANES),
                        batch, seq, min(512, seq))
    y_attn = _mm_act(o_attn, w_attn_out[0].astype(BF16), zeros_d, "none", tm, 512, "attn_out")
    y_conv = _conv_module(u, u_m, conv_dw_w[0], row(conv_dw_b[0]), row(conv_ln_g[0]), row(conv_ln_b[0]),
                          w_conv_out[0].astype(BF16), batch, seq, 256)

    h1, h1b = _mix(y_conv, y_attn, gates, x2, g0, b0, w_out[0].astype(BF16),
                   row(ln1_g[0]), row(ln1_b[0]), 512)

    qp = _mm_act(h1b, peer_w_q[0].astype(BF16), zeros_d, "none", tm, 512, "peer_query")
    out = _peer(qp, peer_subkeys[0].astype(BF16), h1b, h1, peer_u[0].astype(BF16),
                peer_v[0].T.astype(BF16), row(ln2_g[0]), row(ln2_b[0]), 512, 512)
    return out.reshape(batch, seq, d)
```

```python
import functools
import math

import jax
import jax.numpy as jnp
from jax import lax
from jax.experimental import pallas as pl
from jax.experimental.pallas import tpu as pltpu

F32 = jnp.float32
BF16 = jnp.bfloat16

LN_EPS = 1e-5
N_META = 16
CONV_TAPS = 31
N_HEADS = 16
HEAD_DIM = 128
PEER_HEADS = 8
PEER_NKEYS = 128
PEER_TOPK = 16
DN_ALPHA = 2.0 ** 0.25
LANES = 128
SUBLANES = 8
LOG2E = math.log2(math.e)
GATE_ROWS = 32
SCORE_PIECES = 4
VALUE_PIECES = 8
HALO = 32
VMEM_LIMIT = 56 * 1024 * 1024


def _cparams(*sem):
    return pltpu.CompilerParams(dimension_semantics=sem, vmem_limit_bytes=VMEM_LIMIT)


def _ln(x, g, b):
    mu = jnp.mean(x, axis=-1, keepdims=True)
    xc = x - mu
    var = jnp.mean(xc * xc, axis=-1, keepdims=True)
    return xc * lax.rsqrt(var + LN_EPS) * g + b


def _sigmoid(x):
    return 1.0 / (1.0 + jnp.exp(-x))


def _dot(a, b):
    return jnp.dot(a, b, preferred_element_type=F32)


def _dot_nt(a, b):
    return lax.dot_general(a, b, (((1,), (1,)), ((), ())), preferred_element_type=F32)


def _ln_cast_kernel(x_ref, g_ref, b_ref, o_ref):
    o_ref[...] = _ln(x_ref[...], g_ref[...], b_ref[...]).astype(o_ref.dtype)


def _ln_cast(x, g, b, tm):
    m, d = x.shape
    return pl.pallas_call(
        _ln_cast_kernel,
        grid=(m // tm,),
        in_specs=[pl.BlockSpec((tm, d), lambda i: (i, 0)),
                  pl.BlockSpec((1, d), lambda i: (0, 0)),
                  pl.BlockSpec((1, d), lambda i: (0, 0))],
        out_specs=pl.BlockSpec((tm, d), lambda i: (i, 0)),
        out_shape=jax.ShapeDtypeStruct((m, d), BF16),
        compiler_params=_cparams("parallel"),
        name="ln_in",
    )(x, g, b)


def _mm_glu_kernel(x_ref, wa_ref, wg_ref, ba_ref, bg_ref, o_ref):
    x = x_ref[...]
    a = _dot(x, wa_ref[...]) + ba_ref[...]
    g = _dot(x, wg_ref[...]) + bg_ref[...]
    o_ref[...] = (a * _sigmoid(g)).astype(o_ref.dtype)


def _mm_glu(x, w, b, n, tm, tn):
    m, d = x.shape
    nb = n // tn
    return pl.pallas_call(
        _mm_glu_kernel,
        grid=(m // tm, nb),
        in_specs=[pl.BlockSpec((tm, d), lambda i, j: (i, 0)),
                  pl.BlockSpec((d, tn), lambda i, j: (0, j)),
                  pl.BlockSpec((d, tn), lambda i, j: (0, nb + j)),
                  pl.BlockSpec((1, tn), lambda i, j: (0, j)),
                  pl.BlockSpec((1, tn), lambda i, j: (0, nb + j))],
        out_specs=pl.BlockSpec((tm, tn), lambda i, j: (i, j)),
        out_shape=jax.ShapeDtypeStruct((m, n), F32),
        compiler_params=_cparams("parallel", "arbitrary"),
        name="proj_glu",
    )(x, w, w, b, b)


def _mm_heads_kernel(x_ref, w_ref, b_ref, s_ref, o_ref, *, nh):
    acc = (_dot(x_ref[...], w_ref[...]) + b_ref[...]) * s_ref[...]
    for hh in range(nh):
        o_ref[hh] = acc[:, hh * HEAD_DIM:(hh + 1) * HEAD_DIM].astype(o_ref.dtype)


def _mm_heads(x, w, b, scale, col0, n, tm, tn):
    m, d = x.shape
    nh = tn // HEAD_DIM
    cb = col0 // tn
    return pl.pallas_call(
        functools.partial(_mm_heads_kernel, nh=nh),
        grid=(m // tm, n // tn),
        in_specs=[pl.BlockSpec((tm, d), lambda i, j: (i, 0)),
                  pl.BlockSpec((d, tn), lambda i, j: (0, cb + j)),
                  pl.BlockSpec((1, tn), lambda i, j: (0, cb + j)),
                  pl.BlockSpec((1, tn), lambda i, j: (0, j))],
        out_specs=pl.BlockSpec((nh, tm, HEAD_DIM), lambda i, j: (j, i, 0)),
        out_shape=jax.ShapeDtypeStruct((n // HEAD_DIM, m, HEAD_DIM), BF16),
        compiler_params=_cparams("parallel", "arbitrary"),
        name="proj_qkv",
    )(x, w, b, scale)


def _mm_act_kernel(x_ref, w_ref, b_ref, o_ref, *, act):
    y = _dot(x_ref[...], w_ref[...]) + b_ref[...]
    if act == "sigmoid":
        y = _sigmoid(y)
    o_ref[...] = y.astype(o_ref.dtype)


def _mm_act(x, w, b, act, tm, tn, name):
    m, d = x.shape
    n = w.shape[1]
    return pl.pallas_call(
        functools.partial(_mm_act_kernel, act=act),
        grid=(m // tm, n // tn),
        in_specs=[pl.BlockSpec((tm, d), lambda i, j: (i, 0)),
                  pl.BlockSpec((d, tn), lambda i, j: (0, j)),
                  pl.BlockSpec((1, tn), lambda i, j: (0, j))],
        out_specs=pl.BlockSpec((tm, tn), lambda i, j: (i, j)),
        out_shape=jax.ShapeDtypeStruct((m, n), BF16),
        compiler_params=_cparams("parallel", "arbitrary"),
        name=name,
    )(x, w, b)


def _forget_kernel(w_ref, x_ref, b_ref, o_ref):
    z = _dot_nt(w_ref[...], x_ref[...]) + b_ref[...]
    o_ref[...] = jnp.minimum(z, 0.0) - jnp.log1p(jnp.exp(-jnp.abs(z)))


def _forget(wt, x, b, tm):
    m, d = x.shape
    nh = wt.shape[0]
    return pl.pallas_call(
        _forget_kernel,
        grid=(m // tm,),
        in_specs=[pl.BlockSpec((nh, d), lambda i: (0, 0)),
                  pl.BlockSpec((tm, d), lambda i: (i, 0)),
                  pl.BlockSpec((nh, 1), lambda i: (0, 0))],
        out_specs=pl.BlockSpec((nh, tm), lambda i: (0, i)),
        out_shape=jax.ShapeDtypeStruct((nh, m), F32),
        compiler_params=_cparams("parallel"),
        name="forget_logits",
    )(wt, x, b)


def _split_dot(x, tri):
    hi = x.astype(BF16)
    r = x - hi.astype(F32)
    mid = r.astype(BF16)
    lo = (r - mid.astype(F32)).astype(BF16)
    return _dot(hi, tri) + _dot(mid, tri) + _dot(lo, tri)


def _cumsum_kernel(lf_ref, lfm_ref, tri_ref, c_ref, cm_ref, carry_ref, *, cw):
    j = pl.program_id(1)
    cm = _split_dot(lfm_ref[...], tri_ref[0:LANES, 0:LANES])

    @pl.when(j == 0)
    def _():
        carry_ref[...] = cm[:, N_META - 1:N_META]

    c = _split_dot(lf_ref[...], tri_ref[...]) + carry_ref[...]
    c_ref[...] = c
    cm_ref[...] = cm
    carry_ref[...] = c[:, cw - 1:cw]


def _cumsum(lf, lfm_pad, batch, seq, cw):
    nh = lf.shape[0]
    nc = seq // cw
    tri = (lax.broadcasted_iota(jnp.int32, (cw, cw), 0)
           <= lax.broadcasted_iota(jnp.int32, (cw, cw), 1)).astype(BF16)
    return pl.pallas_call(
        functools.partial(_cumsum_kernel, cw=cw),
        grid=(batch, nc),
        in_specs=[pl.BlockSpec((nh, cw), lambda b, j: (0, b * nc + j)),
                  pl.BlockSpec((nh, LANES), lambda b, j: (0, 0)),
                  pl.BlockSpec((cw, cw), lambda b, j: (0, 0))],
        out_specs=[pl.BlockSpec((nh, cw), lambda b, j: (0, b * nc + j)),
                   pl.BlockSpec((nh, LANES), lambda b, j: (0, 0))],
        out_shape=[jax.ShapeDtypeStruct(lf.shape, F32),
                   jax.ShapeDtypeStruct((nh, LANES), F32)],
        scratch_shapes=[pltpu.VMEM((nh, 1), F32)],
        compiler_params=_cparams("arbitrary", "arbitrary"),
        name="forget_cumsum",
    )(lf, lfm_pad, tri)


def _attn_kernel(q_ref, k_ref, v_ref, km_ref, vm_ref, c_ref, cm_ref, o_ref, *, tq, nsub):
    qi = pl.program_id(2)
    hq = tq // nsub
    q0 = pl.multiple_of(qi * tq, tq)
    c0 = c_ref[0, :, pl.ds(q0, LANES)][:, 0:1]
    qs = [q_ref[0, u * hq:(u + 1) * hq, :] for u in range(nsub)]

    def scores(q, k, bias, row0=None):
        s = _dot_nt(q, k) + bias
        if row0 is not None:
            row = lax.broadcasted_iota(jnp.int32, s.shape, 0) + row0
            col = lax.broadcasted_iota(jnp.int32, s.shape, 1)
            s = jnp.where(col <= row, s, -jnp.inf)
        return s

    def absorb(s, v, carry):
        m_new = jnp.max(s, axis=1, keepdims=True)
        if carry is None:
            p = jnp.exp2(s - m_new)
            return m_new, jnp.sum(p, axis=1, keepdims=True), _dot(p.astype(BF16), v)
        m, l, acc = carry
        m_new = jnp.maximum(m, m_new)
        a = jnp.exp2(m - m_new)
        p = jnp.exp2(s - m_new)
        return (m_new, a * l + jnp.sum(p, axis=1, keepdims=True),
                a * acc + _dot(p.astype(BF16), v))

    bias_m = (c0 - cm_ref[0]) * LOG2E
    carries = tuple(absorb(scores(q, km_ref[0], bias_m), vm_ref[0], None) for q in qs)

    def body(j, carries):
        k0 = pl.multiple_of(j * tq, tq)
        k = k_ref[0, pl.ds(k0, tq), :]
        v = v_ref[0, pl.ds(k0, tq), :]
        bias = (c0 - c_ref[0, :, pl.ds(k0, tq)]) * LOG2E
        ss = [scores(q, k, bias) for q in qs]
        return tuple(absorb(s, v, c) for s, c in zip(ss, carries))

    carries = lax.fori_loop(0, qi, body, carries)

    ss = []
    for u in range(nsub):
        nkeys = (u + 1) * hq
        bias = (c0 - c_ref[0, :, pl.ds(q0, nkeys)]) * LOG2E
        ss.append(scores(qs[u], k_ref[0, pl.ds(q0, nkeys), :], bias, row0=u * hq))
    for u in range(nsub):
        _, l, acc = absorb(ss[u], v_ref[0, pl.ds(q0, (u + 1) * hq), :], carries[u])
        o_ref[u * hq:(u + 1) * hq, :] = (acc / l).astype(o_ref.dtype)


def _attention(qkv, qkv_m, c, cm, batch, seq, tq):
    nq = seq // tq
    hh = N_HEADS
    return pl.pallas_call(
        functools.partial(_attn_kernel, tq=tq, nsub=1),
        grid=(batch, hh, nq),
        in_specs=[pl.BlockSpec((1, tq, HEAD_DIM), lambda b, h, i: (h, b * nq + i, 0)),
                  pl.BlockSpec((1, seq, HEAD_DIM), lambda b, h, i: (hh + h, b, 0)),
                  pl.BlockSpec((1, seq, HEAD_DIM), lambda b, h, i: (2 * hh + h, b, 0)),
                  pl.BlockSpec((1, LANES, HEAD_DIM), lambda b, h, i: (hh + h, 0, 0)),
                  pl.BlockSpec((1, LANES, HEAD_DIM), lambda b, h, i: (2 * hh + h, 0, 0)),
                  pl.BlockSpec((1, 1, seq), lambda b, h, i: (h, 0, b)),
                  pl.BlockSpec((1, 1, LANES), lambda b, h, i: (h, 0, 0))],
        out_specs=pl.BlockSpec((tq, HEAD_DIM), lambda b, h, i: (b * nq + i, h)),
        out_shape=jax.ShapeDtypeStruct((batch * seq, hh * HEAD_DIM), BF16),
        compiler_params=_cparams("parallel", "parallel", "arbitrary"),
        name="fox_attention",
    )(qkv, qkv, qkv, qkv_m, qkv_m, c, cm)


def _conv_kernel(cur_ref, prev_ref, um_ref, w_ref, db_ref, g_ref, b_ref, wo_ref, o_ref,
                 win_ref, h_ref, *, ts, rc, cw):
    i = pl.program_id(1)
    d = cur_ref.shape[1]

    @pl.when(i == 0)
    def _():
        win_ref[0:HALO - N_META, :] = jnp.zeros((HALO - N_META, d), F32)
        win_ref[HALO - N_META:HALO, :] = um_ref[...]

    @pl.when(i > 0)
    def _():
        win_ref[0:HALO, :] = prev_ref[...]

    win_ref[HALO:HALO + ts, :] = cur_ref[...]

    off = HALO - (CONV_TAPS - 1)
    nwin = rc + HALO

    def row_chunk(r, _):
        r0 = pl.multiple_of(r * rc, rc)
        for cc in range(d // cw):
            cs = slice(cc * cw, (cc + 1) * cw)
            wd = win_ref[pl.ds(r0, nwin), cs]
            acc = jnp.zeros((rc, cw), F32)
            for ph in range(SUBLANES):
                ws = wd if ph == 0 else pltpu.roll(wd, nwin - ph, axis=0)
                for k in range(CONV_TAPS):
                    if (off + k) % SUBLANES == ph:
                        a0 = off + k - ph
                        acc = acc + w_ref[k:k + 1, cs] * ws[a0:a0 + rc]
            h_ref[pl.ds(r0, rc), cs] = acc + db_ref[:, cs]
        return 0

    lax.fori_loop(0, ts // rc, row_chunk, 0)

    hn = _ln(h_ref[...], g_ref[...], b_ref[...])
    hs = hn * _sigmoid(hn)
    o_ref[...] = _dot(hs.astype(BF16), wo_ref[...]).astype(o_ref.dtype)


def _conv_module(u, u_m, dw_w, dw_b, ln_g, ln_b, w_o, batch, seq, ts):
    d = u.shape[1]
    ns = seq // ts
    hb = ts // HALO
    return pl.pallas_call(
        functools.partial(_conv_kernel, ts=ts, rc=32, cw=256),
        grid=(batch, ns),
        in_specs=[pl.BlockSpec((ts, d), lambda b, i: (b * ns + i, 0)),
                  pl.BlockSpec((HALO, d), lambda b, i: (jnp.maximum((b * ns + i) * hb - 1, 0), 0)),
                  pl.BlockSpec((N_META, d), lambda b, i: (0, 0)),
                  pl.BlockSpec((CONV_TAPS, d), lambda b, i: (0, 0)),
                  pl.BlockSpec((1, d), lambda b, i: (0, 0)),
                  pl.BlockSpec((1, d), lambda b, i: (0, 0)),
                  pl.BlockSpec((1, d), lambda b, i: (0, 0)),
                  pl.BlockSpec((d, d), lambda b, i: (0, 0))],
        out_specs=pl.BlockSpec((ts, d), lambda b, i: (b * ns + i, 0)),
        out_shape=jax.ShapeDtypeStruct((batch * seq, d), BF16),
        scratch_shapes=[pltpu.VMEM((HALO + ts, d), F32), pltpu.VMEM((ts, d), F32)],
        compiler_params=_cparams("parallel", "arbitrary"),
        name="conv_module",
    )(u, u, u_m, dw_w, dw_b, ln_g, ln_b, w_o)


def _mix_kernel(yc_ref, ya_ref, gc_ref, ga_ref, x_ref, g0_ref, b0_ref, wo_ref, g1_ref, b1_ref,
                h_ref, hb_ref):
    m = (gc_ref[...].astype(F32) * yc_ref[...].astype(F32)
         + ga_ref[...].astype(F32) * ya_ref[...].astype(F32))
    mix = _dot(m.astype(BF16), wo_ref[...])
    h0 = _ln(x_ref[...], g0_ref[...], b0_ref[...])
    h1 = _ln(DN_ALPHA * h0 + mix, g1_ref[...], b1_ref[...])
    h_ref[...] = h1
    hb_ref[...] = h1.astype(BF16)


def _mix(yc, ya, gates, x, g0, b0, w_out, g1, b1, tm):
    m, d = x.shape
    row = lambda i: (i, 0)
    vec = pl.BlockSpec((1, d), lambda i: (0, 0))
    return pl.pallas_call(
        _mix_kernel,
        grid=(m // tm,),
        in_specs=[pl.BlockSpec((tm, d), row), pl.BlockSpec((tm, d), row),
                  pl.BlockSpec((tm, d), row), pl.BlockSpec((tm, d), lambda i: (i, 1)),
                  pl.BlockSpec((tm, d), row), vec, vec,
                  pl.BlockSpec((d, d), lambda i: (0, 0)), vec, vec],
        out_specs=[pl.BlockSpec((tm, d), row), pl.BlockSpec((tm, d), row)],
        out_shape=[jax.ShapeDtypeStruct((m, d), F32), jax.ShapeDtypeStruct((m, d), BF16)],
        compiler_params=_cparams("parallel"),
        name="merge_ln1",
    )(yc, ya, gates, gates, x, g0, b0, w_out, g1, b1)


def _kth_largest_rows(work, k, out_ref=None):
    m = None
    for t in range(k):
        m = jnp.max(work, axis=0, keepdims=True)
        if out_ref is not None:
            out_ref[t:t + 1, :] = m
        if t + 1 < k:
            work = jnp.where(work == m, -jnp.inf, work)
    return m


def _interleave(a, b):
    out, ia, ib = [], 0, 0
    while ia < len(a) or ib < len(b):
        if ib >= len(b) or (ia < len(a) and ia * len(b) <= ib * len(a)):
            out.append(a[ia])
            ia += 1
        else:
            out.append(b[ib])
            ib += 1
    return out


def _peer_kernel(qp_ref, sk_ref, h_ref, u_ref, vt_ref, g2_ref, b2_ref, o_ref,
                 s0_ref, s1_ref, e1_ref, f0_ref, tau_ref, sv0_ref, sv1_ref, cand_ref,
                 act0_ref, act1_ref, hc0_ref, hc1_ref, hbt_ref, acc_ref, *, ig, nj):
    acts = (act0_ref, act1_ref)
    hcs = (hc0_ref, hc1_ref)
    j = pl.program_id(1)
    nk = PEER_NKEYS

    @pl.when(j == 0)
    def _():
        acc_ref[...] = jnp.zeros_like(acc_ref)
        hbt_ref[...] = h_ref[...].T.astype(BF16)
        act1_ref[...] = jnp.zeros_like(act1_ref)
        hc0_ref[...] = jnp.zeros_like(hc0_ref)
        for h in range(PEER_HEADS):
            q0 = qp_ref[:, (2 * h) * nk:(2 * h + 1) * nk]
            q1 = qp_ref[:, (2 * h + 1) * nk:(2 * h + 2) * nk]
            s0 = _dot_nt(sk_ref[h, 0], q0)
            s1 = _dot_nt(sk_ref[h, 1], q1)
            _kth_largest_rows(s0, PEER_TOPK, sv0_ref)
            _kth_largest_rows(s1, PEER_TOPK, sv1_ref)
            lo8 = sv1_ref[0:8, :]
            cand_ref[0:16, :] = sv0_ref[0:1, :] + sv1_ref[...]
            cand_ref[16:24, :] = sv0_ref[1:2, :] + lo8
            cand_ref[24:32, :] = sv0_ref[2:3, :] + lo8
            cand_ref[32:40, :] = sv0_ref[3:4, :] + lo8
            cand_ref[40:48, :] = sv0_ref[8:16, :] + sv1_ref[0:1, :]
            for b in range(3):
                cand_ref[48 + 4 * b:52 + 4 * b, :] = sv0_ref[4:8, :] + sv1_ref[b:b + 1, :]
            cand_ref[60:64, :] = jnp.full((4, cand_ref.shape[1]), -jnp.inf, F32)
            cand = cand_ref[...]
            tau = _kth_largest_rows(cand, PEER_TOPK)
            m0 = sv0_ref[0:1, :]
            m1 = sv1_ref[0:1, :]
            z = jnp.sum(jnp.where(cand >= tau, jnp.exp(cand - (m0 + m1)), 0.0), axis=0, keepdims=True)
            f0 = jnp.exp(s0 - m0) / z
            for b in range(nj):
                s0_ref[b, h * ig:(h + 1) * ig, :] = s0[b * ig:(b + 1) * ig, :]
                f0_ref[b, h * ig:(h + 1) * ig, :] = f0[b * ig:(b + 1) * ig, :]
            s1_ref[h] = s1
            e1_ref[h] = jnp.exp(s1 - m1)
            tau_ref[h:h + 1, :] = tau

    def step(par):
        chunk = jnp.clip(j - 1, 0, nj - 1)
        tw = acc_ref.shape[1] // 2

        n_gate = nk // GATE_ROWS
        e_rows = ig * nk // SCORE_PIECES
        d_rows = acc_ref.shape[0] // VALUE_PIECES

        def score_piece(ts, p):
            es = slice(p * e_rows, (p + 1) * e_rows)
            acts[par][es, ts] = _dot(u_ref[es, :], hbt_ref[:, ts])

        def gate_piece(lg, jb):
            ls = slice(lg * LANES, (lg + 1) * LANES)
            rs = slice(jb * GATE_ROWS, (jb + 1) * GATE_ROWS)
            ws = [None] * ig
            for h in range(PEER_HEADS):
                s1 = s1_ref[h, rs, ls]
                e1 = e1_ref[h, rs, ls]
                tau = tau_ref[h:h + 1, ls]
                for ii in range(ig):
                    r = h * ig + ii
                    val = s0_ref[chunk, r:r + 1, ls] + s1
                    wh = jnp.where(val >= tau, e1, 0.0) * f0_ref[chunk, r:r + 1, ls]
                    ws[ii] = wh if ws[ii] is None else ws[ii] + wh
            for ii in range(ig):
                es = slice(ii * nk + jb * GATE_ROWS, ii * nk + (jb + 1) * GATE_ROWS)
                a = acts[1 - par][es, ls]
                gelu = 0.5 * a * (1.0 + lax.erf(a * (1.0 / math.sqrt(2.0))))
                hcs[1 - par][es, ls] = (ws[ii] * gelu).astype(BF16)

        def value_piece(ts, p):
            ds = slice(p * d_rows, (p + 1) * d_rows)
            acc_ref[ds, ts] += _dot(vt_ref[ds, :], hcs[par][:, ts])

        for th in range(2):
            ts = slice(th * tw, (th + 1) * tw)
            gates = [functools.partial(gate_piece, 2 * th + g // n_gate, g % n_gate)
                     for g in range(2 * n_gate)]
            scores = [functools.partial(score_piece, ts, p) for p in range(SCORE_PIECES)]
            values = [functools.partial(value_piece, ts, p) for p in range(VALUE_PIECES)]
            for piece in _interleave(gates, _interleave(scores, values)):
                piece()

    @pl.when(j % 2 == 0)
    def _():
        step(0)

    @pl.when(j % 2 == 1)
    def _():
        step(1)

    @pl.when(j == pl.num_programs(1) - 1)
    def _():
        y = DN_ALPHA * h_ref[...] + acc_ref[...].T
        o_ref[...] = _ln(y, g2_ref[...], b2_ref[...])


def _peer(qp, sk, h, u, vt, g2, b2, tn, ec):
    m, d = h.shape
    ne = u.shape[0]
    ig = ec // PEER_NKEYS
    nj = ne // ec
    row = lambda i, j: (i, 0)
    big = lambda: pltpu.VMEM((PEER_HEADS, PEER_NKEYS, tn), F32)
    rows = lambda: pltpu.VMEM((nj, PEER_HEADS * ig, tn), F32)
    return pl.pallas_call(
        functools.partial(_peer_kernel, ig=ig, nj=nj),
        grid=(m // tn, nj + 2),
        in_specs=[pl.BlockSpec((tn, qp.shape[1]), row),
                  pl.BlockSpec(sk.shape, lambda i, j: (0, 0, 0, 0)),
                  pl.BlockSpec((tn, d), row),
                  pl.BlockSpec((ec, d), lambda i, j: (jnp.minimum(j, nj - 1), 0)),
                  pl.BlockSpec((d, ec), lambda i, j: (0, jnp.clip(j - 2, 0, nj - 1))),
                  pl.BlockSpec((1, d), lambda i, j: (0, 0)),
                  pl.BlockSpec((1, d), lambda i, j: (0, 0))],
        out_specs=pl.BlockSpec((tn, d), row),
        out_shape=jax.ShapeDtypeStruct((m, d), F32),
        scratch_shapes=[rows(), big(), big(), rows(),
                        pltpu.VMEM((PEER_HEADS, tn), F32),
                        pltpu.VMEM((PEER_TOPK, tn), F32),
                        pltpu.VMEM((PEER_TOPK, tn), F32),
                        pltpu.VMEM((64, tn), F32),
                        pltpu.VMEM((ec, tn), F32), pltpu.VMEM((ec, tn), F32),
                        pltpu.VMEM((ec, tn), BF16), pltpu.VMEM((ec, tn), BF16),
                        pltpu.VMEM((d, tn), BF16),
                        pltpu.VMEM((d, tn), F32)],
        compiler_params=_cparams("parallel", "arbitrary"),
        name="peer_dense",
    )(qp, sk, h, u, vt, g2, b2)


def kernel(x, meta_tokens, ln_in_g, ln_in_b, w_in, b_in, conv_dw_w, conv_dw_b, conv_ln_g, conv_ln_b, w_conv_out, w_attn_out, w_out, ln1_g, ln1_b, peer_w_q, peer_subkeys, peer_u, peer_v, ln2_g, ln2_b):
    batch, seq, d = x.shape
    assert w_in.shape[0] == 1, "one layer only"
    assert d == N_HEADS * HEAD_DIM
    t = batch * seq
    off_f = 5 * d
    off_gate = off_f + N_HEADS
    row = lambda v: v.reshape(1, -1)

    w_main = w_in[0][:, :off_f].astype(BF16)
    b_main = row(b_in[0][:off_f])
    w_f_t = w_in[0][:, off_f:off_gate].T.astype(BF16)
    b_f = b_in[0][off_f:off_gate].reshape(N_HEADS, 1)
    w_gate = w_in[0][:, off_gate:].astype(BF16)
    b_gate = row(b_in[0][off_gate:])
    qkv_scale = jnp.concatenate([jnp.full((1, d), LOG2E * HEAD_DIM ** -0.5, F32), jnp.ones((1, 2 * d), F32)], axis=1)
    zeros_d = jnp.zeros((1, d), F32)

    x2 = x.reshape(t, d)
    g0, b0 = row(ln_in_g), row(ln_in_b)
    tm = min(1024, t)

    h0 = _ln_cast(x2, g0, b0, 512)
    h0_m = _ln_cast(meta_tokens, g0, b0, N_META)
    u = _mm_glu(h0, w_main, b_main, d, tm, 512)
    u_m = _mm_glu(h0_m, w_main, b_main, d, N_META, 512)
    qkv = _mm_heads(h0, w_main, b_main, qkv_scale, 2 * d, 3 * d, tm, 512)
    qkv_m = _mm_heads(h0_m, w_main, b_main, qkv_scale, 2 * d, 3 * d, N_META, 512)
    gates = _mm_act(h0, w_gate, b_gate, "sigmoid", tm, 512, "proj_gates")
    logf = _forget(w_f_t, h0, b_f, 512)
    logf_m = _forget(w_f_t, h0_m, b_f, N_META)

    logf_m_pad = jnp.pad(logf_m, ((0, 0), (0, LANES - N_META)))
    c, c_m = _cumsum(logf, logf_m_pad, batch, seq, 512)
    c_m = jnp.where(jnp.arange(LANES)[None, :] < N_META, c_m, jnp.inf)
    qkv_m = jnp.pad(qkv_m, ((0, 0), (0, LANES - N_META), (0, 0)))

    o_attn = _attention(qkv, qkv_m, c.reshape(N_HEADS, 1, t), c_m.reshape(N_HEADS, 1, LANES),
                        batch, seq, min(512, seq))
    y_attn = _mm_act(o_attn, w_attn_out[0].astype(BF16), zeros_d, "none", tm, 512, "attn_out")
    y_conv = _conv_module(u, u_m, conv_dw_w[0], row(conv_dw_b[0]), row(conv_ln_g[0]), row(conv_ln_b[0]),
                          w_conv_out[0].astype(BF16), batch, seq, 256)

    h1, h1b = _mix(y_conv, y_attn, gates, x2, g0, b0, w_out[0].astype(BF16),
                   row(ln1_g[0]), row(ln1_b[0]), 512)

    qp = _mm_act(h1b, peer_w_q[0].astype(BF16), zeros_d, "none", tm, 512, "peer_query")
    out = _peer(qp, peer_subkeys[0].astype(BF16), h1, peer_u[0].astype(BF16),
                peer_v[0].T.astype(BF16), row(ln2_g[0]), row(ln2_b[0]), 512, 512)
    return out.reshape(batch, seq, d)
```

```python
import functools
import math

import jax
import jax.numpy as jnp
from jax import lax
from jax.experimental import pallas as pl
from jax.experimental.pallas import tpu as pltpu

F32 = jnp.float32
BF16 = jnp.bfloat16

LN_EPS = 1e-5
N_META = 16
CONV_TAPS = 31
N_HEADS = 16
HEAD_DIM = 128
PEER_HEADS = 8
PEER_NKEYS = 128
PEER_TOPK = 16
DN_ALPHA = 2.0 ** 0.25
LANES = 128
SUBLANES = 8
LOG2E = math.log2(math.e)
GATE_ROWS = 32
SCORE_PIECES = 4
VALUE_PIECES = 8
HALO = 32
VMEM_LIMIT = 56 * 1024 * 1024


def _cparams(*sem):
    return pltpu.CompilerParams(dimension_semantics=sem, vmem_limit_bytes=VMEM_LIMIT)


def _ln(x, g, b):
    mu = jnp.mean(x, axis=-1, keepdims=True)
    xc = x - mu
    var = jnp.mean(xc * xc, axis=-1, keepdims=True)
    return xc * lax.rsqrt(var + LN_EPS) * g + b


def _sigmoid(x):
    return 1.0 / (1.0 + jnp.exp(-x))


def _dot(a, b):
    return jnp.dot(a, b, preferred_element_type=F32)


def _dot_nt(a, b):
    return lax.dot_general(a, b, (((1,), (1,)), ((), ())), preferred_element_type=F32)


def _ln_cast_kernel(x_ref, g_ref, b_ref, o_ref):
    o_ref[...] = _ln(x_ref[...], g_ref[...], b_ref[...]).astype(o_ref.dtype)


def _ln_cast(x, g, b, tm):
    m, d = x.shape
    return pl.pallas_call(
        _ln_cast_kernel,
        grid=(m // tm,),
        in_specs=[pl.BlockSpec((tm, d), lambda i: (i, 0)),
                  pl.BlockSpec((1, d), lambda i: (0, 0)),
                  pl.BlockSpec((1, d), lambda i: (0, 0))],
        out_specs=pl.BlockSpec((tm, d), lambda i: (i, 0)),
        out_shape=jax.ShapeDtypeStruct((m, d), BF16),
        compiler_params=_cparams("parallel"),
        name="ln_in",
    )(x, g, b)


def _mm_glu_kernel(x_ref, wa_ref, wg_ref, ba_ref, bg_ref, o_ref):
    x = x_ref[...]
    a = _dot(x, wa_ref[...]) + ba_ref[...]
    g = _dot(x, wg_ref[...]) + bg_ref[...]
    o_ref[...] = (a * _sigmoid(g)).astype(o_ref.dtype)


def _mm_glu(x, w, b, n, tm, tn):
    m, d = x.shape
    nb = n // tn
    return pl.pallas_call(
        _mm_glu_kernel,
        grid=(m // tm, nb),
        in_specs=[pl.BlockSpec((tm, d), lambda i, j: (i, 0)),
                  pl.BlockSpec((d, tn), lambda i, j: (0, j)),
                  pl.BlockSpec((d, tn), lambda i, j: (0, nb + j)),
                  pl.BlockSpec((1, tn), lambda i, j: (0, j)),
                  pl.BlockSpec((1, tn), lambda i, j: (0, nb + j))],
        out_specs=pl.BlockSpec((tm, tn), lambda i, j: (i, j)),
        out_shape=jax.ShapeDtypeStruct((m, n), F32),
        compiler_params=_cparams("parallel", "arbitrary"),
        name="proj_glu",
    )(x, w, w, b, b)


def _mm_heads_kernel(x_ref, w_ref, b_ref, s_ref, o_ref, *, nh):
    acc = (_dot(x_ref[...], w_ref[...]) + b_ref[...]) * s_ref[...]
    for hh in range(nh):
        o_ref[hh] = acc[:, hh * HEAD_DIM:(hh + 1) * HEAD_DIM].astype(o_ref.dtype)


def _mm_heads(x, w, b, scale, col0, n, tm, tn):
    m, d = x.shape
    nh = tn // HEAD_DIM
    cb = col0 // tn
    return pl.pallas_call(
        functools.partial(_mm_heads_kernel, nh=nh),
        grid=(m // tm, n // tn),
        in_specs=[pl.BlockSpec((tm, d), lambda i, j: (i, 0)),
                  pl.BlockSpec((d, tn), lambda i, j: (0, cb + j)),
                  pl.BlockSpec((1, tn), lambda i, j: (0, cb + j)),
                  pl.BlockSpec((1, tn), lambda i, j: (0, j))],
        out_specs=pl.BlockSpec((nh, tm, HEAD_DIM), lambda i, j: (j, i, 0)),
        out_shape=jax.ShapeDtypeStruct((n // HEAD_DIM, m, HEAD_DIM), BF16),
        compiler_params=_cparams("parallel", "arbitrary"),
        name="proj_qkv",
    )(x, w, b, scale)


def _mm_act_kernel(x_ref, w_ref, b_ref, o_ref, *, act):
    y = _dot(x_ref[...], w_ref[...]) + b_ref[...]
    if act == "sigmoid":
        y = _sigmoid(y)
    o_ref[...] = y.astype(o_ref.dtype)


def _mm_act(x, w, b, act, tm, tn, name):
    m, d = x.shape
    n = w.shape[1]
    return pl.pallas_call(
        functools.partial(_mm_act_kernel, act=act),
        grid=(m // tm, n // tn),
        in_specs=[pl.BlockSpec((tm, d), lambda i, j: (i, 0)),
                  pl.BlockSpec((d, tn), lambda i, j: (0, j)),
                  pl.BlockSpec((1, tn), lambda i, j: (0, j))],
        out_specs=pl.BlockSpec((tm, tn), lambda i, j: (i, j)),
        out_shape=jax.ShapeDtypeStruct((m, n), BF16),
        compiler_params=_cparams("parallel", "arbitrary"),
        name=name,
    )(x, w, b)


def _forget_kernel(w_ref, x_ref, b_ref, o_ref):
    z = _dot_nt(w_ref[...], x_ref[...]) + b_ref[...]
    o_ref[...] = jnp.minimum(z, 0.0) - jnp.log1p(jnp.exp(-jnp.abs(z)))


def _forget(wt, x, b, tm):
    m, d = x.shape
    nh = wt.shape[0]
    return pl.pallas_call(
        _forget_kernel,
        grid=(m // tm,),
        in_specs=[pl.BlockSpec((nh, d), lambda i: (0, 0)),
                  pl.BlockSpec((tm, d), lambda i: (i, 0)),
                  pl.BlockSpec((nh, 1), lambda i: (0, 0))],
        out_specs=pl.BlockSpec((nh, tm), lambda i: (0, i)),
        out_shape=jax.ShapeDtypeStruct((nh, m), F32),
        compiler_params=_cparams("parallel"),
        name="forget_logits",
    )(wt, x, b)


def _split_dot(x, tri):
    hi = x.astype(BF16)
    r = x - hi.astype(F32)
    mid = r.astype(BF16)
    lo = (r - mid.astype(F32)).astype(BF16)
    return _dot(hi, tri) + _dot(mid, tri) + _dot(lo, tri)


def _cumsum_kernel(lf_ref, lfm_ref, tri_ref, c_ref, cm_ref, carry_ref, *, cw):
    j = pl.program_id(1)
    cm = _split_dot(lfm_ref[...], tri_ref[0:LANES, 0:LANES])

    @pl.when(j == 0)
    def _():
        carry_ref[...] = cm[:, N_META - 1:N_META]

    c = _split_dot(lf_ref[...], tri_ref[...]) + carry_ref[...]
    c_ref[...] = c
    cm_ref[...] = cm
    carry_ref[...] = c[:, cw - 1:cw]


def _cumsum(lf, lfm_pad, batch, seq, cw):
    nh = lf.shape[0]
    nc = seq // cw
    tri = (lax.broadcasted_iota(jnp.int32, (cw, cw), 0)
           <= lax.broadcasted_iota(jnp.int32, (cw, cw), 1)).astype(BF16)
    return pl.pallas_call(
        functools.partial(_cumsum_kernel, cw=cw),
        grid=(batch, nc),
        in_specs=[pl.BlockSpec((nh, cw), lambda b, j: (0, b * nc + j)),
                  pl.BlockSpec((nh, LANES), lambda b, j: (0, 0)),
                  pl.BlockSpec((cw, cw), lambda b, j: (0, 0))],
        out_specs=[pl.BlockSpec((nh, cw), lambda b, j: (0, b * nc + j)),
                   pl.BlockSpec((nh, LANES), lambda b, j: (0, 0))],
        out_shape=[jax.ShapeDtypeStruct(lf.shape, F32),
                   jax.ShapeDtypeStruct((nh, LANES), F32)],
        scratch_shapes=[pltpu.VMEM((nh, 1), F32)],
        compiler_params=_cparams("arbitrary", "arbitrary"),
        name="forget_cumsum",
    )(lf, lfm_pad, tri)


def _attn_kernel(q_ref, k_ref, v_ref, km_ref, vm_ref, c_ref, cm_ref, o_ref, *, tq):
    qi = pl.program_id(2)
    q = q_ref[0]
    q0 = pl.multiple_of(qi * tq, tq)
    c0 = c_ref[0, :, pl.ds(q0, LANES)][:, 0:1]

    def scores(k0, causal=False):
        bias = (c0 - c_ref[0, :, pl.ds(k0, tq)]) * LOG2E
        s = _dot_nt(q, k_ref[0, pl.ds(k0, tq), :]) + bias
        if causal:
            row = lax.broadcasted_iota(jnp.int32, s.shape, 0)
            col = lax.broadcasted_iota(jnp.int32, s.shape, 1)
            s = jnp.where(col <= row, s, -jnp.inf)
        return s

    def absorb(s, v, carry):
        m_new = jnp.max(s, axis=1, keepdims=True)
        if carry is None:
            p = jnp.exp2(s - m_new)
            return m_new, jnp.sum(p, axis=1, keepdims=True), _dot(p.astype(BF16), v)
        m, l, acc = carry
        m_new = jnp.maximum(m, m_new)
        a = jnp.exp2(m - m_new)
        p = jnp.exp2(s - m_new)
        return (m_new, a * l + jnp.sum(p, axis=1, keepdims=True),
                a * acc + _dot(p.astype(BF16), v))

    def pair(k0, k1, carry, causal1):
        s0 = scores(k0)
        s1 = scores(k1, causal1)
        carry = absorb(s0, v_ref[0, pl.ds(k0, tq), :], carry)
        return absorb(s1, v_ref[0, pl.ds(k1, tq), :], carry)

    s_m = _dot_nt(q, km_ref[0]) + (c0 - cm_ref[0]) * LOG2E
    carry = absorb(s_m, vm_ref[0], None)

    def body(jj, carry):
        k0 = pl.multiple_of(2 * jj * tq, tq)
        return pair(k0, k0 + tq, carry, False)

    carry = lax.fori_loop(0, qi // 2, body, carry)

    def tail_odd(carry):
        return pair(pl.multiple_of(q0 - tq, tq), q0, carry, True)

    def tail_even(carry):
        return absorb(scores(q0, True), v_ref[0, pl.ds(q0, tq), :], carry)

    _, l, acc = lax.cond(qi % 2 == 1, tail_odd, tail_even, carry)
    o_ref[...] = (acc / l).astype(o_ref.dtype)


def _attention(qkv, qkv_m, c, cm, batch, seq, tq):
    nq = seq // tq
    hh = N_HEADS
    return pl.pallas_call(
        functools.partial(_attn_kernel, tq=tq),
        grid=(batch, hh, nq),
        in_specs=[pl.BlockSpec((1, tq, HEAD_DIM), lambda b, h, i: (h, b * nq + i, 0)),
                  pl.BlockSpec((1, seq, HEAD_DIM), lambda b, h, i: (hh + h, b, 0)),
                  pl.BlockSpec((1, seq, HEAD_DIM), lambda b, h, i: (2 * hh + h, b, 0)),
                  pl.BlockSpec((1, LANES, HEAD_DIM), lambda b, h, i: (hh + h, 0, 0)),
                  pl.BlockSpec((1, LANES, HEAD_DIM), lambda b, h, i: (2 * hh + h, 0, 0)),
                  pl.BlockSpec((1, 1, seq), lambda b, h, i: (h, 0, b)),
                  pl.BlockSpec((1, 1, LANES), lambda b, h, i: (h, 0, 0))],
        out_specs=pl.BlockSpec((tq, HEAD_DIM), lambda b, h, i: (b * nq + i, h)),
        out_shape=jax.ShapeDtypeStruct((batch * seq, hh * HEAD_DIM), BF16),
        compiler_params=_cparams("parallel", "parallel", "arbitrary"),
        name="fox_attention",
    )(qkv, qkv, qkv, qkv_m, qkv_m, c, cm)


def _conv_kernel(cur_ref, prev_ref, um_ref, w_ref, db_ref, g_ref, b_ref, wo_ref, o_ref,
                 win_ref, h_ref, *, ts, rc, cw):
    i = pl.program_id(1)
    d = cur_ref.shape[1]

    @pl.when(i == 0)
    def _():
        win_ref[0:HALO - N_META, :] = jnp.zeros((HALO - N_META, d), F32)
        win_ref[HALO - N_META:HALO, :] = um_ref[...]

    @pl.when(i > 0)
    def _():
        win_ref[0:HALO, :] = prev_ref[...]

    win_ref[HALO:HALO + ts, :] = cur_ref[...]

    off = HALO - (CONV_TAPS - 1)
    nwin = rc + HALO

    def row_chunk(r, _):
        r0 = pl.multiple_of(r * rc, rc)
        for cc in range(d // cw):
            cs = slice(cc * cw, (cc + 1) * cw)
            wd = win_ref[pl.ds(r0, nwin), cs]
            acc = jnp.zeros((rc, cw), F32)
            for ph in range(SUBLANES):
                ws = wd if ph == 0 else pltpu.roll(wd, nwin - ph, axis=0)
                for k in range(CONV_TAPS):
                    if (off + k) % SUBLANES == ph:
                        a0 = off + k - ph
                        acc = acc + w_ref[k:k + 1, cs] * ws[a0:a0 + rc]
            h_ref[pl.ds(r0, rc), cs] = acc + db_ref[:, cs]
        return 0

    lax.fori_loop(0, ts // rc, row_chunk, 0)

    hn = _ln(h_ref[...], g_ref[...], b_ref[...])
    hs = hn * _sigmoid(hn)
    o_ref[...] = _dot(hs.astype(BF16), wo_ref[...]).astype(o_ref.dtype)


def _conv_module(u, u_m, dw_w, dw_b, ln_g, ln_b, w_o, batch, seq, ts):
    d = u.shape[1]
    ns = seq // ts
    hb = ts // HALO
    return pl.pallas_call(
        functools.partial(_conv_kernel, ts=ts, rc=32, cw=256),
        grid=(batch, ns),
        in_specs=[pl.BlockSpec((ts, d), lambda b, i: (b * ns + i, 0)),
                  pl.BlockSpec((HALO, d), lambda b, i: (jnp.maximum((b * ns + i) * hb - 1, 0), 0)),
                  pl.BlockSpec((N_META, d), lambda b, i: (0, 0)),
                  pl.BlockSpec((CONV_TAPS, d), lambda b, i: (0, 0)),
                  pl.BlockSpec((1, d), lambda b, i: (0, 0)),
                  pl.BlockSpec((1, d), lambda b, i: (0, 0)),
                  pl.BlockSpec((1, d), lambda b, i: (0, 0)),
                  pl.BlockSpec((d, d), lambda b, i: (0, 0))],
        out_specs=pl.BlockSpec((ts, d), lambda b, i: (b * ns + i, 0)),
        out_shape=jax.ShapeDtypeStruct((batch * seq, d), BF16),
        scratch_shapes=[pltpu.VMEM((HALO + ts, d), F32), pltpu.VMEM((ts, d), F32)],
        compiler_params=_cparams("parallel", "arbitrary"),
        name="conv_module",
    )(u, u, u_m, dw_w, dw_b, ln_g, ln_b, w_o)


def _mix_kernel(yc_ref, ya_ref, gc_ref, ga_ref, x_ref, g0_ref, b0_ref, wo_ref, g1_ref, b1_ref,
                h_ref, hb_ref):
    m = (gc_ref[...].astype(F32) * yc_ref[...].astype(F32)
         + ga_ref[...].astype(F32) * ya_ref[...].astype(F32))
    mix = _dot(m.astype(BF16), wo_ref[...])
    h0 = _ln(x_ref[...], g0_ref[...], b0_ref[...])
    h1 = _ln(DN_ALPHA * h0 + mix, g1_ref[...], b1_ref[...])
    h_ref[...] = h1
    hb_ref[...] = h1.astype(BF16)


def _mix(yc, ya, gates, x, g0, b0, w_out, g1, b1, tm):
    m, d = x.shape
    row = lambda i: (i, 0)
    vec = pl.BlockSpec((1, d), lambda i: (0, 0))
    return pl.pallas_call(
        _mix_kernel,
        grid=(m // tm,),
        in_specs=[pl.BlockSpec((tm, d), row), pl.BlockSpec((tm, d), row),
                  pl.BlockSpec((tm, d), row), pl.BlockSpec((tm, d), lambda i: (i, 1)),
                  pl.BlockSpec((tm, d), row), vec, vec,
                  pl.BlockSpec((d, d), lambda i: (0, 0)), vec, vec],
        out_specs=[pl.BlockSpec((tm, d), row), pl.BlockSpec((tm, d), row)],
        out_shape=[jax.ShapeDtypeStruct((m, d), F32), jax.ShapeDtypeStruct((m, d), BF16)],
        compiler_params=_cparams("parallel"),
        name="merge_ln1",
    )(yc, ya, gates, gates, x, g0, b0, w_out, g1, b1)


def _kth_largest_rows(work, k, out_ref=None):
    m = None
    for t in range(k):
        m = jnp.max(work, axis=0, keepdims=True)
        if out_ref is not None:
            out_ref[t:t + 1, :] = m
        if t + 1 < k:
            work = jnp.where(work == m, -jnp.inf, work)
    return m


def _interleave(a, b):
    out, ia, ib = [], 0, 0
    while ia < len(a) or ib < len(b):
        if ib >= len(b) or (ia < len(a) and ia * len(b) <= ib * len(a)):
            out.append(a[ia])
            ia += 1
        else:
            out.append(b[ib])
            ib += 1
    return out


def _peer_kernel(qp_ref, sk_ref, h_ref, u_ref, vt_ref, g2_ref, b2_ref, o_ref,
                 s0_ref, s1_ref, e1_ref, f0_ref, tau_ref, sv0_ref, sv1_ref, cand_ref,
                 act0_ref, act1_ref, hc0_ref, hc1_ref, hbt_ref, acc_ref, *, ig, nj):
    acts = (act0_ref, act1_ref)
    hcs = (hc0_ref, hc1_ref)
    j = pl.program_id(1)
    nk = PEER_NKEYS

    @pl.when(j == 0)
    def _():
        acc_ref[...] = jnp.zeros_like(acc_ref)
        hbt_ref[...] = h_ref[...].T.astype(BF16)
        act1_ref[...] = jnp.zeros_like(act1_ref)
        hc0_ref[...] = jnp.zeros_like(hc0_ref)
        for h in range(PEER_HEADS):
            q0 = qp_ref[:, (2 * h) * nk:(2 * h + 1) * nk]
            q1 = qp_ref[:, (2 * h + 1) * nk:(2 * h + 2) * nk]
            s0 = _dot_nt(sk_ref[h, 0], q0)
            s1 = _dot_nt(sk_ref[h, 1], q1)
            _kth_largest_rows(s0, PEER_TOPK, sv0_ref)
            _kth_largest_rows(s1, PEER_TOPK, sv1_ref)
            lo8 = sv1_ref[0:8, :]
            cand_ref[0:16, :] = sv0_ref[0:1, :] + sv1_ref[...]
            cand_ref[16:24, :] = sv0_ref[1:2, :] + lo8
            cand_ref[24:32, :] = sv0_ref[2:3, :] + lo8
            cand_ref[32:40, :] = sv0_ref[3:4, :] + lo8
            cand_ref[40:48, :] = sv0_ref[8:16, :] + sv1_ref[0:1, :]
            for b in range(3):
                cand_ref[48 + 4 * b:52 + 4 * b, :] = sv0_ref[4:8, :] + sv1_ref[b:b + 1, :]
            cand_ref[60:64, :] = jnp.full((4, cand_ref.shape[1]), -jnp.inf, F32)
            cand = cand_ref[...]
            tau = _kth_largest_rows(cand, PEER_TOPK)
            m0 = sv0_ref[0:1, :]
            m1 = sv1_ref[0:1, :]
            z = jnp.sum(jnp.where(cand >= tau, jnp.exp(cand - (m0 + m1)), 0.0), axis=0, keepdims=True)
            f0 = jnp.exp(s0 - m0) / z
            for b in range(nj):
                s0_ref[b, h * ig:(h + 1) * ig, :] = s0[b * ig:(b + 1) * ig, :]
                f0_ref[b, h * ig:(h + 1) * ig, :] = f0[b * ig:(b + 1) * ig, :]
            s1_ref[h] = s1
            e1_ref[h] = jnp.exp(s1 - m1)
            tau_ref[h:h + 1, :] = tau

    def step(par):
        chunk = jnp.clip(j - 1, 0, nj - 1)
        tw = acc_ref.shape[1] // 2

        n_gate = nk // GATE_ROWS
        e_rows = ig * nk // SCORE_PIECES
        d_rows = acc_ref.shape[0] // VALUE_PIECES

        def score_piece(ts, p):
            es = slice(p * e_rows, (p + 1) * e_rows)
            acts[par][es, ts] = _dot(u_ref[es, :], hbt_ref[:, ts])

        def gate_piece(lg, jb):
            ls = slice(lg * LANES, (lg + 1) * LANES)
            rs = slice(jb * GATE_ROWS, (jb + 1) * GATE_ROWS)
            ws = [None] * ig
            for h in range(PEER_HEADS):
                s1 = s1_ref[h, rs, ls]
                e1 = e1_ref[h, rs, ls]
                tau = tau_ref[h:h + 1, ls]
                for ii in range(ig):
                    r = h * ig + ii
                    val = s0_ref[chunk, r:r + 1, ls] + s1
                    wh = jnp.where(val >= tau, e1, 0.0) * f0_ref[chunk, r:r + 1, ls]
                    ws[ii] = wh if ws[ii] is None else ws[ii] + wh
            for ii in range(ig):
                es = slice(ii * nk + jb * GATE_ROWS, ii * nk + (jb + 1) * GATE_ROWS)
                a = acts[1 - par][es, ls]
                gelu = 0.5 * a * (1.0 + lax.erf(a * (1.0 / math.sqrt(2.0))))
                hcs[1 - par][es, ls] = (ws[ii] * gelu).astype(BF16)

        def value_piece(ts, p):
            ds = slice(p * d_rows, (p + 1) * d_rows)
            acc_ref[ds, ts] += _dot(vt_ref[ds, :], hcs[par][:, ts])

        for th in range(2):
            ts = slice(th * tw, (th + 1) * tw)
            gates = [functools.partial(gate_piece, 2 * th + g // n_gate, g % n_gate)
                     for g in range(2 * n_gate)]
            scores = [functools.partial(score_piece, ts, p) for p in range(SCORE_PIECES)]
            values = [functools.partial(value_piece, ts, p) for p in range(VALUE_PIECES)]
            for piece in _interleave(gates, _interleave(scores, values)):
                piece()

    @pl.when(j % 2 == 0)
    def _():
        step(0)

    @pl.when(j % 2 == 1)
    def _():
        step(1)

    @pl.when(j == pl.num_programs(1) - 1)
    def _():
        y = DN_ALPHA * h_ref[...] + acc_ref[...].T
        o_ref[...] = _ln(y, g2_ref[...], b2_ref[...])


def _peer(qp, sk, h, u, vt, g2, b2, tn, ec):
    m, d = h.shape
    ne = u.shape[0]
    ig = ec // PEER_NKEYS
    nj = ne // ec
    row = lambda i, j: (i, 0)
    big = lambda: pltpu.VMEM((PEER_HEADS, PEER_NKEYS, tn), F32)
    rows = lambda: pltpu.VMEM((nj, PEER_HEADS * ig, tn), F32)
    return pl.pallas_call(
        functools.partial(_peer_kernel, ig=ig, nj=nj),
        grid=(m // tn, nj + 2),
        in_specs=[pl.BlockSpec((tn, qp.shape[1]), row),
                  pl.BlockSpec(sk.shape, lambda i, j: (0, 0, 0, 0)),
                  pl.BlockSpec((tn, d), row),
                  pl.BlockSpec((ec, d), lambda i, j: (jnp.minimum(j, nj - 1), 0)),
                  pl.BlockSpec((d, ec), lambda i, j: (0, jnp.clip(j - 2, 0, nj - 1))),
                  pl.BlockSpec((1, d), lambda i, j: (0, 0)),
                  pl.BlockSpec((1, d), lambda i, j: (0, 0))],
        out_specs=pl.BlockSpec((tn, d), row),
        out_shape=jax.ShapeDtypeStruct((m, d), F32),
        scratch_shapes=[rows(), big(), big(), rows(),
                        pltpu.VMEM((PEER_HEADS, tn), F32),
                        pltpu.VMEM((PEER_TOPK, tn), F32),
                        pltpu.VMEM((PEER_TOPK, tn), F32),
                        pltpu.VMEM((64, tn), F32),
                        pltpu.VMEM((ec, tn), F32), pltpu.VMEM((ec, tn), F32),
                        pltpu.VMEM((ec, tn), BF16), pltpu.VMEM((ec, tn), BF16),
                        pltpu.VMEM((d, tn), BF16),
                        pltpu.VMEM((d, tn), F32)],
        compiler_params=_cparams("parallel", "arbitrary"),
        name="peer_dense",
    )(qp, sk, h, u, vt, g2, b2)


def kernel(x, meta_tokens, ln_in_g, ln_in_b, w_in, b_in, conv_dw_w, conv_dw_b, conv_ln_g, conv_ln_b, w_conv_out, w_attn_out, w_out, ln1_g, ln1_b, peer_w_q, peer_subkeys, peer_u, peer_v, ln2_g, ln2_b):
    batch, seq, d = x.shape
    assert w_in.shape[0] == 1, "one layer only"
    assert d == N_HEADS * HEAD_DIM
    t = batch * seq
    off_f = 5 * d
    off_gate = off_f + N_HEADS
    row = lambda v: v.reshape(1, -1)

    w_main = w_in[0][:, :off_f].astype(BF16)
    b_main = row(b_in[0][:off_f])
    w_f_t = w_in[0][:, off_f:off_gate].T.astype(BF16)
    b_f = b_in[0][off_f:off_gate].reshape(N_HEADS, 1)
    w_gate = w_in[0][:, off_gate:].astype(BF16)
    b_gate = row(b_in[0][off_gate:])
    qkv_scale = jnp.concatenate([jnp.full((1, d), LOG2E * HEAD_DIM ** -0.5, F32), jnp.ones((1, 2 * d), F32)], axis=1)
    zeros_d = jnp.zeros((1, d), F32)

    x2 = x.reshape(t, d)
    g0, b0 = row(ln_in_g), row(ln_in_b)
    tm = min(1024, t)

    h0 = _ln_cast(x2, g0, b0, 512)
    h0_m = _ln_cast(meta_tokens, g0, b0, N_META)
    u = _mm_glu(h0, w_main, b_main, d, tm, 512)
    u_m = _mm_glu(h0_m, w_main, b_main, d, N_META, 512)
    qkv = _mm_heads(h0, w_main, b_main, qkv_scale, 2 * d, 3 * d, tm, 512)
    qkv_m = _mm_heads(h0_m, w_main, b_main, qkv_scale, 2 * d, 3 * d, N_META, 512)
    gates = _mm_act(h0, w_gate, b_gate, "sigmoid", tm, 512, "proj_gates")
    logf = _forget(w_f_t, h0, b_f, 512)
    logf_m = _forget(w_f_t, h0_m, b_f, N_META)

    logf_m_pad = jnp.pad(logf_m, ((0, 0), (0, LANES - N_META)))
    c, c_m = _cumsum(logf, logf_m_pad, batch, seq, 512)
    c_m = jnp.where(jnp.arange(LANES)[None, :] < N_META, c_m, jnp.inf)
    qkv_m = jnp.pad(qkv_m, ((0, 0), (0, LANES - N_META), (0, 0)))

    o_attn = _attention(qkv, qkv_m, c.reshape(N_HEADS, 1, t), c_m.reshape(N_HEADS, 1, LANES),
                        batch, seq, min(512, seq))
    y_attn = _mm_act(o_attn, w_attn_out[0].astype(BF16), zeros_d, "none", tm, 512, "attn_out")
    y_conv = _conv_module(u, u_m, conv_dw_w[0], row(conv_dw_b[0]), row(conv_ln_g[0]), row(conv_ln_b[0]),
                          w_conv_out[0].astype(BF16), batch, seq, 256)

    h1, h1b = _mix(y_conv, y_attn, gates, x2, g0, b0, w_out[0].astype(BF16),
                   row(ln1_g[0]), row(ln1_b[0]), 512)

    qp = _mm_act(h1b, peer_w_q[0].astype(BF16), zeros_d, "none", tm, 512, "peer_query")
    out = _peer(qp, peer_subkeys[0].astype(BF16), h1, peer_u[0].astype(BF16),
                peer_v[0].T.astype(BF16), row(ln2_g[0]), row(ln2_b[0]), 512, 512)
    return out.reshape(batch, seq, d)
```

```python
import functools
import math

import jax
import jax.numpy as jnp
from jax import lax
from jax.experimental import pallas as pl
from jax.experimental.pallas import tpu as pltpu

F32 = jnp.float32
BF16 = jnp.bfloat16

LN_EPS = 1e-5
N_META = 16
CONV_TAPS = 31
N_HEADS = 16
HEAD_DIM = 128
PEER_HEADS = 8
PEER_NKEYS = 128
PEER_TOPK = 16
DN_ALPHA = 2.0 ** 0.25
LANES = 128
SUBLANES = 8
LOG2E = math.log2(math.e)
KV_UNROLL = 4
GATE_ROWS = 32
SCORE_PIECES = 4
VALUE_PIECES = 8
HALO = 32
VMEM_LIMIT = 56 * 1024 * 1024


def _cparams(*sem):
    return pltpu.CompilerParams(dimension_semantics=sem, vmem_limit_bytes=VMEM_LIMIT)


def _ln(x, g, b):
    mu = jnp.mean(x, axis=-1, keepdims=True)
    xc = x - mu
    var = jnp.mean(xc * xc, axis=-1, keepdims=True)
    return xc * lax.rsqrt(var + LN_EPS) * g + b


def _sigmoid(x):
    return 1.0 / (1.0 + jnp.exp(-x))


def _dot(a, b):
    return jnp.dot(a, b, preferred_element_type=F32)


def _dot_nt(a, b):
    return lax.dot_general(a, b, (((1,), (1,)), ((), ())), preferred_element_type=F32)


def _ln_cast_kernel(x_ref, g_ref, b_ref, o_ref):
    o_ref[...] = _ln(x_ref[...], g_ref[...], b_ref[...]).astype(o_ref.dtype)


def _ln_cast(x, g, b, tm):
    m, d = x.shape
    return pl.pallas_call(
        _ln_cast_kernel,
        grid=(m // tm,),
        in_specs=[pl.BlockSpec((tm, d), lambda i: (i, 0)),
                  pl.BlockSpec((1, d), lambda i: (0, 0)),
                  pl.BlockSpec((1, d), lambda i: (0, 0))],
        out_specs=pl.BlockSpec((tm, d), lambda i: (i, 0)),
        out_shape=jax.ShapeDtypeStruct((m, d), BF16),
        compiler_params=_cparams("parallel"),
        name="ln_in",
    )(x, g, b)


def _mm_glu_kernel(x_ref, wa_ref, wg_ref, ba_ref, bg_ref, o_ref):
    x = x_ref[...]
    a = _dot(x, wa_ref[...]) + ba_ref[...]
    g = _dot(x, wg_ref[...]) + bg_ref[...]
    o_ref[...] = (a * _sigmoid(g)).astype(o_ref.dtype)


def _mm_glu(x, w, b, n, tm, tn):
    m, d = x.shape
    nb = n // tn
    return pl.pallas_call(
        _mm_glu_kernel,
        grid=(m // tm, nb),
        in_specs=[pl.BlockSpec((tm, d), lambda i, j: (i, 0)),
                  pl.BlockSpec((d, tn), lambda i, j: (0, j)),
                  pl.BlockSpec((d, tn), lambda i, j: (0, nb + j)),
                  pl.BlockSpec((1, tn), lambda i, j: (0, j)),
                  pl.BlockSpec((1, tn), lambda i, j: (0, nb + j))],
        out_specs=pl.BlockSpec((tm, tn), lambda i, j: (i, j)),
        out_shape=jax.ShapeDtypeStruct((m, n), F32),
        compiler_params=_cparams("parallel", "arbitrary"),
        name="proj_glu",
    )(x, w, w, b, b)


def _mm_heads_kernel(x_ref, w_ref, b_ref, s_ref, o_ref, *, nh):
    acc = (_dot(x_ref[...], w_ref[...]) + b_ref[...]) * s_ref[...]
    for hh in range(nh):
        o_ref[hh] = acc[:, hh * HEAD_DIM:(hh + 1) * HEAD_DIM].astype(o_ref.dtype)


def _mm_heads(x, w, b, scale, col0, n, tm, tn):
    m, d = x.shape
    nh = tn // HEAD_DIM
    cb = col0 // tn
    return pl.pallas_call(
        functools.partial(_mm_heads_kernel, nh=nh),
        grid=(m // tm, n // tn),
        in_specs=[pl.BlockSpec((tm, d), lambda i, j: (i, 0)),
                  pl.BlockSpec((d, tn), lambda i, j: (0, cb + j)),
                  pl.BlockSpec((1, tn), lambda i, j: (0, cb + j)),
                  pl.BlockSpec((1, tn), lambda i, j: (0, j))],
        out_specs=pl.BlockSpec((nh, tm, HEAD_DIM), lambda i, j: (j, i, 0)),
        out_shape=jax.ShapeDtypeStruct((n // HEAD_DIM, m, HEAD_DIM), BF16),
        compiler_params=_cparams("parallel", "arbitrary"),
        name="proj_qkv",
    )(x, w, b, scale)


def _mm_act_kernel(x_ref, w_ref, b_ref, o_ref, *, act):
    y = _dot(x_ref[...], w_ref[...]) + b_ref[...]
    if act == "sigmoid":
        y = _sigmoid(y)
    o_ref[...] = y.astype(o_ref.dtype)


def _mm_act(x, w, b, act, tm, tn, name):
    m, d = x.shape
    n = w.shape[1]
    return pl.pallas_call(
        functools.partial(_mm_act_kernel, act=act),
        grid=(m // tm, n // tn),
        in_specs=[pl.BlockSpec((tm, d), lambda i, j: (i, 0)),
                  pl.BlockSpec((d, tn), lambda i, j: (0, j)),
                  pl.BlockSpec((1, tn), lambda i, j: (0, j))],
        out_specs=pl.BlockSpec((tm, tn), lambda i, j: (i, j)),
        out_shape=jax.ShapeDtypeStruct((m, n), BF16),
        compiler_params=_cparams("parallel", "arbitrary"),
        name=name,
    )(x, w, b)


def _forget_kernel(w_ref, x_ref, b_ref, o_ref):
    z = _dot_nt(w_ref[...], x_ref[...]) + b_ref[...]
    o_ref[...] = jnp.minimum(z, 0.0) - jnp.log1p(jnp.exp(-jnp.abs(z)))


def _forget(wt, x, b, tm):
    m, d = x.shape
    nh = wt.shape[0]
    return pl.pallas_call(
        _forget_kernel,
        grid=(m // tm,),
        in_specs=[pl.BlockSpec((nh, d), lambda i: (0, 0)),
                  pl.BlockSpec((tm, d), lambda i: (i, 0)),
                  pl.BlockSpec((nh, 1), lambda i: (0, 0))],
        out_specs=pl.BlockSpec((nh, tm), lambda i: (0, i)),
        out_shape=jax.ShapeDtypeStruct((nh, m), F32),
        compiler_params=_cparams("parallel"),
        name="forget_logits",
    )(wt, x, b)


def _split_dot(x, tri):
    hi = x.astype(BF16)
    r = x - hi.astype(F32)
    mid = r.astype(BF16)
    lo = (r - mid.astype(F32)).astype(BF16)
    return _dot(hi, tri) + _dot(mid, tri) + _dot(lo, tri)


def _cumsum_kernel(lf_ref, lfm_ref, tri_ref, c_ref, cm_ref, carry_ref, *, cw):
    j = pl.program_id(1)
    cm = _split_dot(lfm_ref[...], tri_ref[0:LANES, 0:LANES])

    @pl.when(j == 0)
    def _():
        carry_ref[...] = cm[:, N_META - 1:N_META]

    c = _split_dot(lf_ref[...], tri_ref[...]) + carry_ref[...]
    c_ref[...] = c
    cm_ref[...] = cm
    carry_ref[...] = c[:, cw - 1:cw]


def _cumsum(lf, lfm_pad, batch, seq, cw):
    nh = lf.shape[0]
    nc = seq // cw
    tri = (lax.broadcasted_iota(jnp.int32, (cw, cw), 0)
           <= lax.broadcasted_iota(jnp.int32, (cw, cw), 1)).astype(BF16)
    return pl.pallas_call(
        functools.partial(_cumsum_kernel, cw=cw),
        grid=(batch, nc),
        in_specs=[pl.BlockSpec((nh, cw), lambda b, j: (0, b * nc + j)),
                  pl.BlockSpec((nh, LANES), lambda b, j: (0, 0)),
                  pl.BlockSpec((cw, cw), lambda b, j: (0, 0))],
        out_specs=[pl.BlockSpec((nh, cw), lambda b, j: (0, b * nc + j)),
                   pl.BlockSpec((nh, LANES), lambda b, j: (0, 0))],
        out_shape=[jax.ShapeDtypeStruct(lf.shape, F32),
                   jax.ShapeDtypeStruct((nh, LANES), F32)],
        scratch_shapes=[pltpu.VMEM((nh, 1), F32)],
        compiler_params=_cparams("arbitrary", "arbitrary"),
        name="forget_cumsum",
    )(lf, lfm_pad, tri)


def _attn_kernel(q_ref, k_ref, v_ref, km_ref, vm_ref, c_ref, cm_ref, o_ref, *, tq):
    qi = pl.program_id(2)
    q = q_ref[0]
    q0 = pl.multiple_of(qi * tq, tq)
    c0 = c_ref[0, :, pl.ds(q0, LANES)][:, 0:1]

    def scores(k0, causal=False):
        bias = (c0 - c_ref[0, :, pl.ds(k0, tq)]) * LOG2E
        s = _dot_nt(q, k_ref[0, pl.ds(k0, tq), :]) + bias
        if causal:
            row = lax.broadcasted_iota(jnp.int32, s.shape, 0)
            col = lax.broadcasted_iota(jnp.int32, s.shape, 1)
            s = jnp.where(col <= row, s, -jnp.inf)
        return s

    def absorb(s, v, carry):
        m_new = jnp.max(s, axis=1, keepdims=True)
        if carry is None:
            p = jnp.exp2(s - m_new)
            return m_new, jnp.sum(p, axis=1, keepdims=True), _dot(p.astype(BF16), v)
        m, l, acc = carry
        m_new = jnp.maximum(m, m_new)
        a = jnp.exp2(m - m_new)
        p = jnp.exp2(s - m_new)
        return (m_new, a * l + jnp.sum(p, axis=1, keepdims=True),
                a * acc + _dot(p.astype(BF16), v))

    def group(k0s, carry, causal_last):
        ss = [scores(k0, causal_last and n == len(k0s) - 1) for n, k0 in enumerate(k0s)]
        for s, k0 in zip(ss, k0s):
            carry = absorb(s, v_ref[0, pl.ds(k0, tq), :], carry)
        return carry

    s_m = _dot_nt(q, km_ref[0]) + (c0 - cm_ref[0]) * LOG2E
    carry = absorb(s_m, vm_ref[0], None)

    def body(jj, carry):
        k0 = pl.multiple_of(KV_UNROLL * jj * tq, tq)
        return group([k0 + n * tq for n in range(KV_UNROLL)], carry, False)

    carry = lax.fori_loop(0, qi // KV_UNROLL, body, carry)

    def tail(r):
        def run(carry):
            return group([pl.multiple_of(q0 - (r - n) * tq, tq) for n in range(r + 1)], carry, True)
        return run

    _, l, acc = lax.switch(qi % KV_UNROLL, [tail(r) for r in range(KV_UNROLL)], carry)
    o_ref[...] = (acc / l).astype(o_ref.dtype)


def _attention(qkv, qkv_m, c, cm, batch, seq, tq):
    nq = seq // tq
    hh = N_HEADS
    return pl.pallas_call(
        functools.partial(_attn_kernel, tq=tq),
        grid=(batch, hh, nq),
        in_specs=[pl.BlockSpec((1, tq, HEAD_DIM), lambda b, h, i: (h, b * nq + i, 0)),
                  pl.BlockSpec((1, seq, HEAD_DIM), lambda b, h, i: (hh + h, b, 0)),
                  pl.BlockSpec((1, seq, HEAD_DIM), lambda b, h, i: (2 * hh + h, b, 0)),
                  pl.BlockSpec((1, LANES, HEAD_DIM), lambda b, h, i: (hh + h, 0, 0)),
                  pl.BlockSpec((1, LANES, HEAD_DIM), lambda b, h, i: (2 * hh + h, 0, 0)),
                  pl.BlockSpec((1, 1, seq), lambda b, h, i: (h, 0, b)),
                  pl.BlockSpec((1, 1, LANES), lambda b, h, i: (h, 0, 0))],
        out_specs=pl.BlockSpec((tq, HEAD_DIM), lambda b, h, i: (b * nq + i, h)),
        out_shape=jax.ShapeDtypeStruct((batch * seq, hh * HEAD_DIM), BF16),
        compiler_params=_cparams("parallel", "parallel", "arbitrary"),
        name="fox_attention",
    )(qkv, qkv, qkv, qkv_m, qkv_m, c, cm)


def _conv_kernel(cur_ref, prev_ref, um_ref, w_ref, db_ref, g_ref, b_ref, wo_ref, o_ref,
                 win_ref, h_ref, *, ts, rc, cw):
    i = pl.program_id(1)
    d = cur_ref.shape[1]

    @pl.when(i == 0)
    def _():
        win_ref[0:HALO - N_META, :] = jnp.zeros((HALO - N_META, d), F32)
        win_ref[HALO - N_META:HALO, :] = um_ref[...]

    @pl.when(i > 0)
    def _():
        win_ref[0:HALO, :] = prev_ref[...]

    win_ref[HALO:HALO + ts, :] = cur_ref[...]

    off = HALO - (CONV_TAPS - 1)
    nwin = rc + HALO

    def row_chunk(r, _):
        r0 = pl.multiple_of(r * rc, rc)
        for cc in range(d // cw):
            cs = slice(cc * cw, (cc + 1) * cw)
            wd = win_ref[pl.ds(r0, nwin), cs]
            acc = jnp.zeros((rc, cw), F32)
            for ph in range(SUBLANES):
                ws = wd if ph == 0 else pltpu.roll(wd, nwin - ph, axis=0)
                for k in range(CONV_TAPS):
                    if (off + k) % SUBLANES == ph:
                        a0 = off + k - ph
                        acc = acc + w_ref[k:k + 1, cs] * ws[a0:a0 + rc]
            h_ref[pl.ds(r0, rc), cs] = acc + db_ref[:, cs]
        return 0

    lax.fori_loop(0, ts // rc, row_chunk, 0)

    hn = _ln(h_ref[...], g_ref[...], b_ref[...])
    hs = hn * _sigmoid(hn)
    o_ref[...] = _dot(hs.astype(BF16), wo_ref[...]).astype(o_ref.dtype)


def _conv_module(u, u_m, dw_w, dw_b, ln_g, ln_b, w_o, batch, seq, ts):
    d = u.shape[1]
    ns = seq // ts
    hb = ts // HALO
    return pl.pallas_call(
        functools.partial(_conv_kernel, ts=ts, rc=32, cw=256),
        grid=(batch, ns),
        in_specs=[pl.BlockSpec((ts, d), lambda b, i: (b * ns + i, 0)),
                  pl.BlockSpec((HALO, d), lambda b, i: (jnp.maximum((b * ns + i) * hb - 1, 0), 0)),
                  pl.BlockSpec((N_META, d), lambda b, i: (0, 0)),
                  pl.BlockSpec((CONV_TAPS, d), lambda b, i: (0, 0)),
                  pl.BlockSpec((1, d), lambda b, i: (0, 0)),
                  pl.BlockSpec((1, d), lambda b, i: (0, 0)),
                  pl.BlockSpec((1, d), lambda b, i: (0, 0)),
                  pl.BlockSpec((d, d), lambda b, i: (0, 0))],
        out_specs=pl.BlockSpec((ts, d), lambda b, i: (b * ns + i, 0)),
        out_shape=jax.ShapeDtypeStruct((batch * seq, d), BF16),
        scratch_shapes=[pltpu.VMEM((HALO + ts, d), F32), pltpu.VMEM((ts, d), F32)],
        compiler_params=_cparams("parallel", "arbitrary"),
        name="conv_module",
    )(u, u, u_m, dw_w, dw_b, ln_g, ln_b, w_o)


def _mix_kernel(yc_ref, ya_ref, gc_ref, ga_ref, x_ref, g0_ref, b0_ref, wo_ref, g1_ref, b1_ref,
                h_ref, hb_ref):
    m = (gc_ref[...].astype(F32) * yc_ref[...].astype(F32)
         + ga_ref[...].astype(F32) * ya_ref[...].astype(F32))
    mix = _dot(m.astype(BF16), wo_ref[...])
    h0 = _ln(x_ref[...], g0_ref[...], b0_ref[...])
    h1 = _ln(DN_ALPHA * h0 + mix, g1_ref[...], b1_ref[...])
    h_ref[...] = h1
    hb_ref[...] = h1.astype(BF16)


def _mix(yc, ya, gates, x, g0, b0, w_out, g1, b1, tm):
    m, d = x.shape
    row = lambda i: (i, 0)
    vec = pl.BlockSpec((1, d), lambda i: (0, 0))
    return pl.pallas_call(
        _mix_kernel,
        grid=(m // tm,),
        in_specs=[pl.BlockSpec((tm, d), row), pl.BlockSpec((tm, d), row),
                  pl.BlockSpec((tm, d), row), pl.BlockSpec((tm, d), lambda i: (i, 1)),
                  pl.BlockSpec((tm, d), row), vec, vec,
                  pl.BlockSpec((d, d), lambda i: (0, 0)), vec, vec],
        out_specs=[pl.BlockSpec((tm, d), row), pl.BlockSpec((tm, d), row)],
        out_shape=[jax.ShapeDtypeStruct((m, d), F32), jax.ShapeDtypeStruct((m, d), BF16)],
        compiler_params=_cparams("parallel"),
        name="merge_ln1",
    )(yc, ya, gates, gates, x, g0, b0, w_out, g1, b1)


def _kth_largest_rows(work, k, out_ref=None):
    m = None
    for t in range(k):
        m = jnp.max(work, axis=0, keepdims=True)
        if out_ref is not None:
            out_ref[t:t + 1, :] = m
        if t + 1 < k:
            work = jnp.where(work == m, -jnp.inf, work)
    return m


def _interleave(a, b):
    out, ia, ib = [], 0, 0
    while ia < len(a) or ib < len(b):
        if ib >= len(b) or (ia < len(a) and ia * len(b) <= ib * len(a)):
            out.append(a[ia])
            ia += 1
        else:
            out.append(b[ib])
            ib += 1
    return out


def _peer_kernel(qp_ref, sk_ref, h_ref, u_ref, vt_ref, g2_ref, b2_ref, o_ref,
                 s0_ref, s1_ref, e1_ref, f0_ref, tau_ref, sv0_ref, sv1_ref, cand_ref,
                 act0_ref, act1_ref, hc0_ref, hc1_ref, hbt_ref, acc_ref, *, ig, nj):
    acts = (act0_ref, act1_ref)
    hcs = (hc0_ref, hc1_ref)
    j = pl.program_id(1)
    nk = PEER_NKEYS

    @pl.when(j == 0)
    def _():
        acc_ref[...] = jnp.zeros_like(acc_ref)
        hbt_ref[...] = h_ref[...].T.astype(BF16)
        for h in range(PEER_HEADS):
            q0 = qp_ref[:, (2 * h) * nk:(2 * h + 1) * nk]
            q1 = qp_ref[:, (2 * h + 1) * nk:(2 * h + 2) * nk]
            s0 = _dot_nt(sk_ref[h, 0], q0)
            s1 = _dot_nt(sk_ref[h, 1], q1)
            _kth_largest_rows(s0, PEER_TOPK, sv0_ref)
            _kth_largest_rows(s1, PEER_TOPK, sv1_ref)
            lo8 = sv1_ref[0:8, :]
            cand_ref[0:16, :] = sv0_ref[0:1, :] + sv1_ref[...]
            cand_ref[16:24, :] = sv0_ref[1:2, :] + lo8
            cand_ref[24:32, :] = sv0_ref[2:3, :] + lo8
            cand_ref[32:40, :] = sv0_ref[3:4, :] + lo8
            cand_ref[40:48, :] = sv0_ref[8:16, :] + sv1_ref[0:1, :]
            for b in range(3):
                cand_ref[48 + 4 * b:52 + 4 * b, :] = sv0_ref[4:8, :] + sv1_ref[b:b + 1, :]
            cand_ref[60:64, :] = jnp.full((4, cand_ref.shape[1]), -jnp.inf, F32)
            cand = cand_ref[...]
            tau = _kth_largest_rows(cand, PEER_TOPK)
            m0 = sv0_ref[0:1, :]
            m1 = sv1_ref[0:1, :]
            z = jnp.sum(jnp.where(cand >= tau, jnp.exp(cand - (m0 + m1)), 0.0), axis=0, keepdims=True)
            f0 = jnp.exp(s0 - m0) / z
            for b in range(nj):
                s0_ref[b, h * ig:(h + 1) * ig, :] = s0[b * ig:(b + 1) * ig, :]
                f0_ref[b, h * ig:(h + 1) * ig, :] = f0[b * ig:(b + 1) * ig, :]
            s1_ref[h] = s1
            e1_ref[h] = jnp.exp(s1 - m1)
            tau_ref[h:h + 1, :] = tau

    def step(par, do_score=True, do_gate=True, do_value=True):
        chunk = jnp.clip(j - 1, 0, nj - 1)
        tw = acc_ref.shape[1] // 2

        n_gate = nk // GATE_ROWS
        e_rows = ig * nk // SCORE_PIECES
        d_rows = acc_ref.shape[0] // VALUE_PIECES

        def score_piece(ts, p):
            es = slice(p * e_rows, (p + 1) * e_rows)
            acts[par][es, ts] = _dot(u_ref[es, :], hbt_ref[:, ts])

        def gate_piece(lg, jb):
            ls = slice(lg * LANES, (lg + 1) * LANES)
            rs = slice(jb * GATE_ROWS, (jb + 1) * GATE_ROWS)
            ws = [None] * ig
            for h in range(PEER_HEADS):
                s1 = s1_ref[h, rs, ls]
                e1 = e1_ref[h, rs, ls]
                tau = tau_ref[h:h + 1, ls]
                for ii in range(ig):
                    r = h * ig + ii
                    val = s0_ref[chunk, r:r + 1, ls] + s1
                    wh = jnp.where(val >= tau, e1, 0.0) * f0_ref[chunk, r:r + 1, ls]
                    ws[ii] = wh if ws[ii] is None else ws[ii] + wh
            for ii in range(ig):
                es = slice(ii * nk + jb * GATE_ROWS, ii * nk + (jb + 1) * GATE_ROWS)
                a = acts[1 - par][es, ls]
                gelu = 0.5 * a * (1.0 + lax.erf(a * (1.0 / math.sqrt(2.0))))
                hcs[1 - par][es, ls] = (ws[ii] * gelu).astype(BF16)

        def value_piece(ts, p):
            ds = slice(p * d_rows, (p + 1) * d_rows)
            acc_ref[ds, ts] += _dot(vt_ref[ds, :], hcs[par][:, ts])

        for th in range(2):
            ts = slice(th * tw, (th + 1) * tw)
            gates = [functools.partial(gate_piece, 2 * th + g // n_gate, g % n_gate)
                     for g in range(2 * n_gate if do_gate else 0)]
            scores = [functools.partial(score_piece, ts, p) for p in range(SCORE_PIECES if do_score else 0)]
            values = [functools.partial(value_piece, ts, p) for p in range(VALUE_PIECES if do_value else 0)]
            for piece in _interleave(gates, _interleave(scores, values)):
                piece()

    steady = jnp.logical_and(j >= 2, j < nj)

    @pl.when(jnp.logical_and(steady, j % 2 == 0))
    def _():
        step(0)

    @pl.when(jnp.logical_and(steady, j % 2 == 1))
    def _():
        step(1)

    @pl.when(j == 0)
    def _():
        step(0, do_gate=False, do_value=False)

    @pl.when(j == 1)
    def _():
        step(1, do_value=False)

    @pl.when(j == nj)
    def _():
        step(nj % 2, do_score=False)

    @pl.when(j == nj + 1)
    def _():
        step((nj + 1) % 2, do_score=False, do_gate=False)

    @pl.when(j == pl.num_programs(1) - 1)
    def _():
        y = DN_ALPHA * h_ref[...] + acc_ref[...].T
        o_ref[...] = _ln(y, g2_ref[...], b2_ref[...])


def _peer(qp, sk, h, u, vt, g2, b2, tn, ec):
    m, d = h.shape
    ne = u.shape[0]
    ig = ec // PEER_NKEYS
    nj = ne // ec
    row = lambda i, j: (i, 0)
    big = lambda: pltpu.VMEM((PEER_HEADS, PEER_NKEYS, tn), F32)
    rows = lambda: pltpu.VMEM((nj, PEER_HEADS * ig, tn), F32)
    return pl.pallas_call(
        functools.partial(_peer_kernel, ig=ig, nj=nj),
        grid=(m // tn, nj + 2),
        in_specs=[pl.BlockSpec((tn, qp.shape[1]), row),
                  pl.BlockSpec(sk.shape, lambda i, j: (0, 0, 0, 0)),
                  pl.BlockSpec((tn, d), row),
                  pl.BlockSpec((ec, d), lambda i, j: (jnp.minimum(j, nj - 1), 0)),
                  pl.BlockSpec((d, ec), lambda i, j: (0, jnp.clip(j - 2, 0, nj - 1))),
                  pl.BlockSpec((1, d), lambda i, j: (0, 0)),
                  pl.BlockSpec((1, d), lambda i, j: (0, 0))],
        out_specs=pl.BlockSpec((tn, d), row),
        out_shape=jax.ShapeDtypeStruct((m, d), F32),
        scratch_shapes=[rows(), big(), big(), rows(),
                        pltpu.VMEM((PEER_HEADS, tn), F32),
                        pltpu.VMEM((PEER_TOPK, tn), F32),
                        pltpu.VMEM((PEER_TOPK, tn), F32),
                        pltpu.VMEM((64, tn), F32),
                        pltpu.VMEM((ec, tn), F32), pltpu.VMEM((ec, tn), F32),
                        pltpu.VMEM((ec, tn), BF16), pltpu.VMEM((ec, tn), BF16),
                        pltpu.VMEM((d, tn), BF16),
                        pltpu.VMEM((d, tn), F32)],
        compiler_params=_cparams("parallel", "arbitrary"),
        name="peer_dense",
    )(qp, sk, h, u, vt, g2, b2)


def kernel(x, meta_tokens, ln_in_g, ln_in_b, w_in, b_in, conv_dw_w, conv_dw_b, conv_ln_g, conv_ln_b, w_conv_out, w_attn_out, w_out, ln1_g, ln1_b, peer_w_q, peer_subkeys, peer_u, peer_v, ln2_g, ln2_b):
    batch, seq, d = x.shape
    assert w_in.shape[0] == 1, "one layer only"
    assert d == N_HEADS * HEAD_DIM
    t = batch * seq
    off_f = 5 * d
    off_gate = off_f + N_HEADS
    row = lambda v: v.reshape(1, -1)

    w_main = w_in[0][:, :off_f].astype(BF16)
    b_main = row(b_in[0][:off_f])
    w_f_t = w_in[0][:, off_f:off_gate].T.astype(BF16)
    b_f = b_in[0][off_f:off_gate].reshape(N_HEADS, 1)
    w_gate = w_in[0][:, off_gate:].astype(BF16)
    b_gate = row(b_in[0][off_gate:])
    qkv_scale = jnp.concatenate([jnp.full((1, d), LOG2E * HEAD_DIM ** -0.5, F32), jnp.ones((1, 2 * d), F32)], axis=1)
    zeros_d = jnp.zeros((1, d), F32)

    x2 = x.reshape(t, d)
    g0, b0 = row(ln_in_g), row(ln_in_b)
    tm = min(1024, t)

    h0 = _ln_cast(x2, g0, b0, 512)
    h0_m = _ln_cast(meta_tokens, g0, b0, N_META)
    u = _mm_glu(h0, w_main, b_main, d, tm, 512)
    u_m = _mm_glu(h0_m, w_main, b_main, d, N_META, 512)
    qkv = _mm_heads(h0, w_main, b_main, qkv_scale, 2 * d, 3 * d, tm, 512)
    qkv_m = _mm_heads(h0_m, w_main, b_main, qkv_scale, 2 * d, 3 * d, N_META, 512)
    gates = _mm_act(h0, w_gate, b_gate, "sigmoid", tm, 512, "proj_gates")
    logf = _forget(w_f_t, h0, b_f, 512)
    logf_m = _forget(w_f_t, h0_m, b_f, N_META)

    logf_m_pad = jnp.pad(logf_m, ((0, 0), (0, LANES - N_META)))
    c, c_m = _cumsum(logf, logf_m_pad, batch, seq, 512)
    c_m = jnp.where(jnp.arange(LANES)[None, :] < N_META, c_m, jnp.inf)
    qkv_m = jnp.pad(qkv_m, ((0, 0), (0, LANES - N_META), (0, 0)))

    o_attn = _attention(qkv, qkv_m, c.reshape(N_HEADS, 1, t), c_m.reshape(N_HEADS, 1, LANES),
                        batch, seq, min(512, seq))
    y_attn = _mm_act(o_attn, w_attn_out[0].astype(BF16), zeros_d, "none", tm, 512, "attn_out")
    y_conv = _conv_module(u, u_m, conv_dw_w[0], row(conv_dw_b[0]), row(conv_ln_g[0]), row(conv_ln_b[0]),
                          w_conv_out[0].astype(BF16), batch, seq, 256)

    h1, h1b = _mix(y_conv, y_attn, gates, x2, g0, b0, w_out[0].astype(BF16),
                   row(ln1_g[0]), row(ln1_b[0]), 512)

    qp = _mm_act(h1b, peer_w_q[0].astype(BF16), zeros_d, "none", tm, 512, "peer_query")
    out = _peer(qp, peer_subkeys[0].astype(BF16), h1, peer_u[0].astype(BF16),
                peer_v[0].T.astype(BF16), row(ln2_g[0]), row(ln2_b[0]), 512, 512)
    return out.reshape(batch, seq, d)
```
